```python
import math
import jax, jax.numpy as jnp
from jax import lax
import numpy as np

D_MODEL = 1024
BATCH = 1
SEQ = 16384
DEPTH = 1
DEC_BATCH = 128
DEC_SEQ = 1
PAST_LEN = 8192
PAGE_SIZE = 128

N_HEADS = 8
HEAD_DIM = 64
ATT_WIDTH = N_HEADS * HEAD_DIM
IDX_HEADS = 16
IDX_DIM = 64
TOPK_MAX = 256
Q_BLOCK = 128
SSM_GROUP = 16
SSM_WIDTH = 512
SSM_GROUPS = SSM_WIDTH // SSM_GROUP
SSM_STATE = 64
PEER_HEADS = 8
PEER_NKEYS = 128
PEER_EXPERTS = PEER_NKEYS * PEER_NKEYS
PEER_KEY_DIM = 128
PEER_HALF = PEER_KEY_DIM // 2
PEER_TOPK = 16
PEER_BLOCK = 128
PLE_DIM = 256
EPS = 1e-6
IN_WIDTH = 3 * ATT_WIDTH + IDX_HEADS * IDX_DIM + IDX_DIM + IDX_HEADS + SSM_WIDTH + 2 * D_MODEL

kernel_name = 'dsa_s5_peer_gated_hybrid_step'


def _split_points():
    widths = [ATT_WIDTH, ATT_WIDTH, ATT_WIDTH, IDX_HEADS * IDX_DIM, IDX_DIM, IDX_HEADS, SSM_WIDTH, D_MODEL]
    return [int(c) for c in np.cumsum(widths)]


def rmsnorm(x, g):
    xf = x.astype(jnp.float32)
    y = xf * lax.rsqrt(jnp.mean(xf * xf, axis=-1, keepdims=True) + EPS)
    return (y * g.astype(jnp.float32)).astype(x.dtype)


def indexer_scores(qi, wi, ki):
    dots = jnp.einsum('bthd,bld->bthl', qi, ki, preferred_element_type=jnp.float32) * IDX_DIM ** -0.5
    return jnp.einsum('bth,bthl->btl', wi.astype(jnp.float32) * IDX_HEADS ** -0.5, jax.nn.relu(dots))


def sparse_softmax_attention(q, kg, vg, valid):
    s = jnp.einsum('bthd,btkhd->bthk', q, kg, preferred_element_type=jnp.float32) * HEAD_DIM ** -0.5
    s = jnp.where(valid[:, :, None, :], s, -jnp.inf)
    pr = jax.nn.softmax(s, axis=-1).astype(vg.dtype)
    return jnp.einsum('bthk,btkhd->bthd', pr, vg)


def prompt_sparse_attention(q, k, v, qi, wi, ki, topk):
    B, S = q.shape[0], q.shape[1]
    n_blocks = S // Q_BLOCK
    key_pos = jnp.arange(S)
    bidx = jnp.arange(B)[:, None, None]

    def block(i):
        start = i * Q_BLOCK
        qb = lax.dynamic_slice_in_dim(q, start, Q_BLOCK, axis=1)
        qib = lax.dynamic_slice_in_dim(qi, start, Q_BLOCK, axis=1)
        wib = lax.dynamic_slice_in_dim(wi, start, Q_BLOCK, axis=1)
        qpos = start + jnp.arange(Q_BLOCK)
        sc = indexer_scores(qib, wib, ki)
        causal = key_pos[None, :] <= qpos[:, None]
        sc = jnp.where(causal[None], sc, -jnp.inf)
        _, idx = lax.top_k(sc, topk)
        valid = idx <= qpos[None, :, None]
        return sparse_softmax_attention(qb, k[bidx, idx], v[bidx, idx], valid)

    out = lax.map(block, jnp.arange(n_blocks))
    return out.transpose(1, 0, 2, 3, 4).reshape(B, S, N_HEADS, HEAD_DIM)


def sample_sparse_attention(q, k_new, v_new, qi, wi, ki_new, cache_k, cache_v, cache_ik, page_table, layer, topk):
    Bd, T = q.shape[0], q.shape[1]
    past = page_table.shape[1] * PAGE_SIZE
    bidx = jnp.arange(Bd)[:, None, None]
    ki_past = cache_ik[layer, page_table].reshape(Bd, past, IDX_DIM)
    ki_all = jnp.concatenate([ki_past, ki_new], axis=1)
    sc = indexer_scores(qi, wi, ki_all)
    qpos = past + jnp.arange(T)
    kpos = jnp.arange(past + T)
    sc = jnp.where(kpos[None, None, :] <= qpos[None, :, None], sc, -jnp.inf)
    _, idx = lax.top_k(sc, topk)
    valid = idx <= qpos[None, :, None]
    ip = jnp.minimum(idx, past - 1)
    phys = page_table[bidx, ip // PAGE_SIZE]
    off = ip % PAGE_SIZE
    in_past = (idx < past)[..., None, None]
    inew = jnp.clip(idx - past, 0, T - 1)
    kg = jnp.where(in_past, cache_k[layer, phys, off], k_new[bidx, inew])
    vg = jnp.where(in_past, cache_v[layer, phys, off], v_new[bidx, inew])
    return sparse_softmax_attention(q, kg, vg, valid)


def s5_discretize(a_re, a_im, log_dt, b_re, b_im):
    dt = jnp.exp(log_dt.astype(jnp.float32))[:, None]
    ar = a_re.astype(jnp.float32)
    ai = a_im.astype(jnp.float32)
    mag = jnp.exp(dt * ar)
    abar_re = mag * jnp.cos(dt * ai)
    abar_im = mag * jnp.sin(dt * ai)
    den = ar * ar + ai * ai
    x_re = abar_re - 1.0
    f_re = (x_re * ar + abar_im * ai) / den
    f_im = (abar_im * ar - x_re * ai) / den
    br = b_re.astype(jnp.float32)
    bi = b_im.astype(jnp.float32)
    bb_re = f_re[..., None] * br - f_im[..., None] * bi
    bb_im = f_re[..., None] * bi + f_im[..., None] * br
    return abar_re, abar_im, bb_re, bb_im


def _complex_affine_combine(e1, e2):
    a1r, a1i, b1r, b1i = e1
    a2r, a2i, b2r, b2i = e2
    return (a1r * a2r - a1i * a2i,
            a1r * a2i + a1i * a2r,
            a2r * b1r - a2i * b1i + b2r,
            a2r * b1i + a2i * b1r + b2i)


def s5_branch(us, h0_re, h0_im, a_re, a_im, log_dt, b_re, b_im, c_re, c_im, d):
    B, T = us.shape[0], us.shape[1]
    u = us.astype(jnp.float32).reshape(B, T, SSM_GROUPS, SSM_GROUP)
    abar_re, abar_im, bb_re, bb_im = s5_discretize(a_re, a_im, log_dt, b_re, b_im)
    bu_re = jnp.einsum('btgc,gpc->btgp', u, bb_re)
    bu_im = jnp.einsum('btgc,gpc->btgp', u, bb_im)
    h0r = h0_re.astype(jnp.float32)
    h0i = h0_im.astype(jnp.float32)
    bu_re = bu_re.at[:, 0].add(abar_re * h0r - abar_im * h0i)
    bu_im = bu_im.at[:, 0].add(abar_re * h0i + abar_im * h0r)
    a_r = jnp.broadcast_to(abar_re, bu_re.shape)
    a_i = jnp.broadcast_to(abar_im, bu_im.shape)
    _, _, h_re, h_im = lax.associative_scan(_complex_affine_combine, (a_r, a_i, bu_re, bu_im), axis=1)
    y = (jnp.einsum('btgp,gcp->btgc', h_re, c_re.astype(jnp.float32))
         - jnp.einsum('btgp,gcp->btgc', h_im, c_im.astype(jnp.float32))
         + d.astype(jnp.float32).reshape(SSM_GROUPS, SSM_GROUP) * u)
    return y, h_re[:, -1], h_im[:, -1]


def peer(x, w_q, sub_keys, u_tab, v_tab):
    n = x.shape[0]
    pad = (-n) % PEER_BLOCK
    xb = jnp.pad(x, ((0, pad), (0, 0))).reshape(-1, PEER_BLOCK, D_MODEL)

    def block(xt):
        q = (xt @ w_q).reshape(PEER_BLOCK, PEER_HEADS, 2, PEER_HALF)
        s = jnp.einsum('thjd,jnd->thjn', q, sub_keys, preferred_element_type=jnp.float32)
        sv, si = lax.top_k(s, PEER_TOPK)
        cand = (sv[:, :, 0, :, None] + sv[:, :, 1, None, :]).reshape(PEER_BLOCK, PEER_HEADS, PEER_TOPK * PEER_TOPK)
        cid = (si[:, :, 0, :, None] * PEER_NKEYS + si[:, :, 1, None, :]).reshape(PEER_BLOCK, PEER_HEADS, PEER_TOPK * PEER_TOPK)
        cs, ci = lax.top_k(cand, PEER_TOPK)
        eid = jnp.take_along_axis(cid, ci, axis=-1)
        g = jax.nn.softmax(cs, axis=-1)
        ug = u_tab[eid]
        act = jax.nn.gelu(jnp.einsum('thkd,td->thk', ug, xt, preferred_element_type=jnp.float32), approximate=False)
        vg = v_tab[eid]
        return jnp.einsum('thk,thkd->td', (g * act).astype(vg.dtype), vg)

    return lax.map(block, xb).reshape(-1, D_MODEL)[:n]


def trunk_layer(h, p_l, attn_fn, h0_re, h0_im, lw):
    B, T = h.shape[0], h.shape[1]
    u = rmsnorm(h, lw['norm_mix'])
    z = u @ lw['w_in']
    q, k, v, qi, ki, wi, us, gate_a, gate_b = jnp.split(z, _split_points(), axis=-1)
    q = q.reshape(B, T, N_HEADS, HEAD_DIM)
    k = k.reshape(B, T, N_HEADS, HEAD_DIM)
    v = v.reshape(B, T, N_HEADS, HEAD_DIM)
    qi = qi.reshape(B, T, IDX_HEADS, IDX_DIM)
    att = attn_fn(q, k, v, qi, wi, ki).reshape(B, T, ATT_WIDTH)
    branch_a = att @ lw['w_att_br']
    y_ssm, s_re, s_im = s5_branch(us, h0_re, h0_im, lw['a_re'], lw['a_im'], lw['log_dt'],
                                  lw['b_re'], lw['b_im'], lw['c_re'], lw['c_im'], lw['d'])
    zs = jax.nn.gelu(y_ssm.reshape(B, T, SSM_WIDTH), approximate=False).astype(h.dtype)
    glu_lin, glu_gate = jnp.split(zs @ lw['w_glu'], 2, axis=-1)
    branch_b = glu_lin * jax.nn.sigmoid(glu_gate)
    mixed = jax.nn.sigmoid(gate_a) * branch_a + jax.nn.sigmoid(gate_b) * branch_b
    h = h + mixed @ lw['w_out']
    un = rmsnorm(h, lw['norm_ffn'])
    h = h + peer(un.reshape(B * T, D_MODEL), lw['peer_wq'], lw['peer_keys'], lw['peer_u'], lw['peer_v']).reshape(B, T, D_MODEL)
    gate = jax.nn.sigmoid(rmsnorm(h, lw['norm_ple']) @ lw['ple_gate'])
    h = h + gate * (p_l @ lw['ple_proj'])
    return h, (k, v, ki), (s_re, s_im)


def setup_inputs(seed: int = 0) -> dict:
    key = jax.random.key(seed)
    ks = iter(jax.random.split(key, 40))

    def nrm(shape, scale):
        return jax.random.normal(next(ks), shape, jnp.float32) * scale

    n_pages = PAST_LEN // PAGE_SIZE
    n_used = DEC_BATCH * n_pages
    n_phys = n_used + (n_used + 3) // 4
    G, P, C = SSM_GROUPS, SSM_STATE, SSM_GROUP
    inp = {}
    inp['x_prompt'] = nrm((BATCH, SEQ, D_MODEL), 1.0)
    inp['x_sample'] = nrm((DEC_BATCH, DEC_SEQ, D_MODEL), 1.0)
    inp['p_prompt'] = nrm((DEPTH, BATCH, SEQ, PLE_DIM), 1.0)
    inp['p_sample'] = nrm((DEPTH, DEC_BATCH, DEC_SEQ, PLE_DIM), 1.0)
    inp['cache_k'] = nrm((DEPTH, n_phys, PAGE_SIZE, N_HEADS, HEAD_DIM), 1.0)
    inp['cache_v'] = nrm((DEPTH, n_phys, PAGE_SIZE, N_HEADS, HEAD_DIM), 1.0)
    inp['cache_idx_k'] = nrm((DEPTH, n_phys, PAGE_SIZE, IDX_DIM), 1.0)
    inp['state_ssm_re'] = nrm((DEPTH, DEC_BATCH, G, P), 0.1)
    inp['state_ssm_im'] = nrm((DEPTH, DEC_BATCH, G, P), 0.1)
    inp['page_table'] = jax.random.permutation(next(ks), n_phys)[:n_used].reshape(DEC_BATCH, n_pages).astype(jnp.int32)
    inp['norm_mix'] = 1.0 + nrm((DEPTH, D_MODEL), 0.01)
    inp['w_in'] = nrm((DEPTH, D_MODEL, IN_WIDTH), D_MODEL ** -0.5)
    inp['ssm_a_re'] = -0.5 + nrm((DEPTH, G, P), 0.01)
    inp['ssm_a_im'] = math.pi * jnp.arange(P, dtype=jnp.float32) + nrm((DEPTH, G, P), 0.01)
    inp['ssm_log_dt'] = jax.random.uniform(next(ks), (DEPTH, G), jnp.float32, math.log(1e-3), math.log(1e-1))
    inp['ssm_b_re'] = nrm((DEPTH, G, P, C), (2.0 * C) ** -0.5)
    inp['ssm_b_im'] = nrm((DEPTH, G, P, C), (2.0 * C) ** -0.5)
    inp['ssm_c_re'] = nrm((DEPTH, G, C, P), (2.0 * P) ** -0.5)
    inp['ssm_c_im'] = nrm((DEPTH, G, C, P), (2.0 * P) ** -0.5)
    inp['ssm_d'] = nrm((DEPTH, SSM_WIDTH), 1.0)
    inp['w_glu'] = nrm((DEPTH, SSM_WIDTH, 2 * D_MODEL), SSM_WIDTH ** -0.5)
    inp['w_att_br'] = nrm((DEPTH, ATT_WIDTH, D_MODEL), ATT_WIDTH ** -0.5)
    inp['w_out'] = nrm((DEPTH, D_MODEL, D_MODEL), D_MODEL ** -0.5)
    inp['norm_ffn'] = 1.0 + nrm((DEPTH, D_MODEL), 0.01)
    inp['peer_wq'] = nrm((DEPTH, D_MODEL, PEER_HEADS * PEER_KEY_DIM), D_MODEL ** -0.5)
    inp['peer_keys'] = nrm((DEPTH, 2, PEER_NKEYS, PEER_HALF), PEER_HALF ** -0.5)
    inp['peer_u'] = nrm((DEPTH, PEER_EXPERTS, D_MODEL), D_MODEL ** -0.5)
    inp['peer_v'] = nrm((DEPTH, PEER_EXPERTS, D_MODEL), 0.1)
    inp['norm_ple'] = 1.0 + nrm((DEPTH, D_MODEL), 0.01)
    inp['ple_gate'] = nrm((DEPTH, D_MODEL, D_MODEL), D_MODEL ** -0.5)
    inp['ple_proj'] = nrm((DEPTH, PLE_DIM, D_MODEL), PLE_DIM ** -0.5)
    inp['norm_final'] = 1.0 + nrm((D_MODEL,), 0.01)
    return inp


def reference(x_prompt, x_sample, p_prompt, p_sample, cache_k, cache_v, cache_idx_k,
              state_ssm_re, state_ssm_im, page_table, norm_mix, w_in, ssm_a_re, ssm_a_im,
              ssm_log_dt, ssm_b_re, ssm_b_im, ssm_c_re, ssm_c_im, ssm_d, w_glu, w_att_br,
              w_out, norm_ffn, peer_wq, peer_keys, peer_u, peer_v, norm_ple, ple_gate,
              ple_proj, norm_final):
    topk_prompt = min(TOPK_MAX, x_prompt.shape[1] // 4)
    topk_sample = min(TOPK_MAX, (page_table.shape[1] * PAGE_SIZE + x_sample.shape[1]) // 4)
    hp, hs = x_prompt, x_sample
    kp, vp, ikp, ksm, vsm, iks = [], [], [], [], [], []
    srp, sip, srs, sis = [], [], [], []
    for l in range(DEPTH):
        lw = dict(norm_mix=norm_mix[l], w_in=w_in[l], a_re=ssm_a_re[l], a_im=ssm_a_im[l],
                  log_dt=ssm_log_dt[l], b_re=ssm_b_re[l], b_im=ssm_b_im[l], c_re=ssm_c_re[l],
                  c_im=ssm_c_im[l], d=ssm_d[l], w_glu=w_glu[l], w_att_br=w_att_br[l],
                  w_out=w_out[l], norm_ffn=norm_ffn[l], peer_wq=peer_wq[l], peer_keys=peer_keys[l],
                  peer_u=peer_u[l], peer_v=peer_v[l], norm_ple=norm_ple[l], ple_gate=ple_gate[l],
                  ple_proj=ple_proj[l])

        def attn_prompt(q, k, v, qi, wi, ki):
            return prompt_sparse_attention(q, k, v, qi, wi, ki, topk_prompt)

        def attn_sample(q, k, v, qi, wi, ki, layer=l):
            return sample_sparse_attention(q, k, v, qi, wi, ki, cache_k, cache_v, cache_idx_k,
                                           page_table, layer, topk_sample)

        zeros = jnp.zeros((hp.shape[0], SSM_GROUPS, SSM_STATE), jnp.float32)
        hp, (k1, v1, i1), (r1, m1) = trunk_layer(hp, p_prompt[l], attn_prompt, zeros, zeros, lw)
        hs, (k2, v2, i2), (r2, m2) = trunk_layer(hs, p_sample[l], attn_sample, state_ssm_re[l], state_ssm_im[l], lw)
        kp.append(k1); vp.append(v1); ikp.append(i1); srp.append(r1); sip.append(m1)
        ksm.append(k2); vsm.append(v2); iks.append(i2); srs.append(r2); sis.append(m2)
    y_prompt = rmsnorm(hp, norm_final)
    y_sample = rmsnorm(hs, norm_final)
    return (y_prompt, y_sample, jnp.stack(kp), jnp.stack(vp), jnp.stack(ikp),
            jnp.stack(ksm), jnp.stack(vsm), jnp.stack(iks),
            jnp.stack(srp), jnp.stack(sip), jnp.stack(srs), jnp.stack(sis))
```

```python
import functools
import math

import jax
import jax.numpy as jnp
import numpy as np
from jax import lax
from jax.experimental import pallas as pl
from jax.experimental.pallas import tpu as pltpu

F32 = jnp.float32
BF16 = jnp.bfloat16
I32 = jnp.int32

D_MODEL = 1024
PAGE_SIZE = 128
N_HEADS = 8
HEAD_DIM = 64
ATT_WIDTH = N_HEADS * HEAD_DIM
IDX_HEADS = 16
IDX_DIM = 64
TOPK_MAX = 256
SSM_GROUP = 16
SSM_WIDTH = 512
SSM_GROUPS = SSM_WIDTH // SSM_GROUP
SSM_STATE = 64
SSM_FLAT = SSM_GROUPS * SSM_STATE
PEER_HEADS = 8
PEER_NKEYS = 128
PEER_EXPERTS = PEER_NKEYS * PEER_NKEYS
PEER_HALF = 64
PEER_TOPK = 16
EPS = 1e-6

LANES = 128
SUBLANES = 8
VMEM_LIMIT = 56 * 1024 * 1024
INT_MIN = -(2 ** 31)
NEG_INF = float("-inf")


def _params(sem, vmem=VMEM_LIMIT):
    return pltpu.CompilerParams(dimension_semantics=sem, vmem_limit_bytes=vmem)


def _split(x):
    hi = x.astype(BF16)
    lo = (x - hi.astype(F32)).astype(BF16)
    return hi, lo


def _dot(a, b):
    return jnp.dot(a, b, preferred_element_type=F32)


def _dot_nt(a, b):
    return lax.dot_general(a, b, (((1,), (1,)), ((), ())), preferred_element_type=F32)


def _rmsnorm(x, g):
    return x * lax.rsqrt(jnp.mean(x * x, axis=-1, keepdims=True) + EPS) * g


def _sigmoid(x):
    return 1.0 / (1.0 + jnp.exp(-x))


def _gelu(x):
    return 0.5 * x * (1.0 + lax.erf(x * (2.0 ** -0.5)))


def _sort_key(x):
    b = lax.bitcast_convert_type(x, I32)
    return b ^ ((b >> 31) & 0x7FFFFFFF)


_PREC_COLS = (IDX_HEADS * IDX_DIM, LANES, LANES, SSM_WIDTH)
_FAST_COLS = (ATT_WIDTH, ATT_WIDTH, ATT_WIDTH, D_MODEL, D_MODEL)


def _inproj_prec_body(x_ref, g_ref, whi_ref, wlo_ref, qi_ref, ka_ref, wb_ref, us_ref, ki4_ref):
    u = _rmsnorm(x_ref[...], g_ref[...])
    uh, ul = _split(u)

    def proj(off, w):
        wh = whi_ref[:, off:off + w]
        return _dot(uh, wh) + _dot(ul, wh) + _dot(uh, wlo_ref[:, off:off + w])

    qi_ref[...] = proj(0, 1024)
    ka = proj(1024, LANES)
    ka_ref[...] = ka
    kh, kl = _split(ka)
    ki4_ref[:, 0:LANES] = kh
    ki4_ref[:, LANES:2 * LANES] = kl
    wb_ref[...] = proj(1024 + LANES, LANES)
    us_ref[...] = proj(1024 + 2 * LANES, SSM_WIDTH)


def _inproj_fast_body(x_ref, g_ref, w_ref, q_ref, k_ref, v_ref, ga_ref, gb_ref, kbf_ref, vt_ref):
    u = _rmsnorm(x_ref[...], g_ref[...]).astype(BF16)
    q_ref[...] = _dot(u, w_ref[:, 0:512])
    k = _dot(u, w_ref[:, 512:1024])
    k_ref[...] = k
    kbf_ref[...] = k.astype(BF16)
    v = _dot(u, w_ref[:, 1024:1536])
    v_ref[...] = v
    vt_ref[...] = v.T.astype(BF16)
    ga_ref[...] = _dot(u, w_ref[:, 1536:2560])
    gb_ref[...] = _dot(u, w_ref[:, 2560:3584])


def _in_projection(x, g, w_prec_hi, w_prec_lo, w_fast, tb):
    t = x.shape[0]
    grid = (t // tb,)
    row = lambda n: pl.BlockSpec((tb, n), lambda i: (i, 0))
    full = lambda a: pl.BlockSpec(a.shape, lambda i: (0,) * a.ndim)
    qi, ka, wb, us, ki4 = pl.pallas_call(
        _inproj_prec_body,
        grid=grid,
        in_specs=[row(D_MODEL), full(g), full(w_prec_hi), full(w_prec_lo)],
        out_specs=[row(1024), row(LANES), row(LANES), row(SSM_WIDTH), row(2 * LANES)],
        out_shape=[jax.ShapeDtypeStruct((t, 1024), F32), jax.ShapeDtypeStruct((t, LANES), F32),
                   jax.ShapeDtypeStruct((t, LANES), F32), jax.ShapeDtypeStruct((t, SSM_WIDTH), F32),
                   jax.ShapeDtypeStruct((t, 2 * LANES), BF16)],
        compiler_params=_params(("arbitrary",)),
        name="inproj_prec",
    )(x, g, w_prec_hi, w_prec_lo)
    q, k, v, ga, gb, kbf, vt = pl.pallas_call(
        _inproj_fast_body,
        grid=grid,
        in_specs=[row(D_MODEL), full(g), full(w_fast)],
        out_specs=[row(512), row(512), row(512), row(1024), row(1024), row(512),
                   pl.BlockSpec((512, tb), lambda i: (0, i))],
        out_shape=[jax.ShapeDtypeStruct((t, 512), F32)] * 3 + [jax.ShapeDtypeStruct((t, 1024), F32)] * 2
        + [jax.ShapeDtypeStruct((t, 512), BF16), jax.ShapeDtypeStruct((512, t), BF16)],
        compiler_params=_params(("arbitrary",)),
        name="inproj_fast",
    )(x, g, w_fast)
    return dict(qi=qi, ka=ka, wb=wb, us=us, ki4=ki4, q=q, k=k, v=v, ga=ga, gb=gb, kbf=kbf, vt=vt)


def _prep_in_weights(w_in):
    o = np.cumsum([0, 512, 512, 512, 1024, 64, 16, 512, 1024, 1024])
    wq, wk, wv, wqi, wki, wwi, wus, wga, wgb = [w_in[:, o[j]:o[j + 1]] for j in range(9)]
    wwi_pad = jnp.pad(wwi, ((0, 0), (0, LANES - IDX_HEADS)))
    w_prec = jnp.concatenate([wqi, wki, wki, wwi_pad, wus], axis=1)
    hi = w_prec.astype(BF16)
    lo = (w_prec - hi.astype(F32)).astype(BF16)
    w_fast = jnp.concatenate([wq, wk, wv, wga, wgb], axis=1).astype(BF16)
    return hi, lo, w_fast


def _kth_largest_key(st_ref, ntiles, topk):
    def count_ge(cand):
        cb = jnp.broadcast_to(cand, (SUBLANES, LANES))

        def body(t, accs):
            base = pl.multiple_of(t * LANES, LANES)
            accs = list(accs)
            for r in range(LANES // SUBLANES):
                v = st_ref[pl.ds(base + r * SUBLANES, SUBLANES), :]
                accs[r % 4] = accs[r % 4] + (v >= cb).astype(I32)
            return tuple(accs)

        accs = lax.fori_loop(0, ntiles, body, (jnp.zeros((SUBLANES, LANES), I32),) * 4)
        return jnp.sum(accs[0] + accs[1] + accs[2] + accs[3], axis=0, keepdims=True)

    def bit_step(b, res):
        bit = jnp.left_shift(jnp.int32(1), 31 - b)
        cand = jnp.where(b == 0, jnp.zeros_like(res), res | bit)
        return jnp.where(count_ge(cand) >= topk, cand, res)

    return lax.fori_loop(0, 32, bit_step, jnp.full((1, LANES), INT_MIN, I32))


ATT_QB = 128
ATT_KC = 512


def _attn_prompt_body(qi_ref, wb_ref, q_ref, ki4_ref, k_ref, vt_ref, o_ref,
                      st_ref, qstat_ref, qbd_ref, w_ref, acc_ref, m_ref, l_ref, *, topk):
    i = pl.program_id(0)

    @pl.when(i == 0)
    def _():
        qbd_ref[...] = jnp.zeros_like(qbd_ref)

    qi_t = qi_ref[...].T
    for h in range(IDX_HEADS):
        xh, xl = _split(qi_t[IDX_DIM * h:IDX_DIM * (h + 1)])
        c = slice((h % 2) * LANES, (h % 2 + 1) * LANES)
        for r, piece in enumerate((xh, xl, xh, xl)):
            qstat_ref[h // 2, IDX_DIM * r:IDX_DIM * (r + 1), c] = piece
    w_ref[...] = wb_ref[...].T[0:IDX_HEADS] * (IDX_DIM ** -0.5 * IDX_HEADS ** -0.5)
    q_t = q_ref[...].T * HEAD_DIM ** -0.5
    for h in range(N_HEADS):
        half = h % 2
        qbd_ref[h // 2, HEAD_DIM * half:HEAD_DIM * (half + 1), LANES * half:LANES * (half + 1)] = (
            q_t[HEAD_DIM * h:HEAD_DIM * (h + 1)].astype(BF16))

    n_chunks = ((i + 1) * ATT_QB + ATT_KC - 1) // ATT_KC

    def score_chunk(c, masked):
        r0 = pl.multiple_of(c * ATT_KC, ATT_KC)
        ks = ki4_ref[pl.ds(r0, ATT_KC), :]
        acc = jnp.zeros((ATT_KC, LANES), F32)
        for p in range(IDX_HEADS // 2):
            d = _dot(ks, qstat_ref[p])
            acc = acc + w_ref[2 * p:2 * p + 1, :] * jnp.maximum(d[:, 0:LANES], 0.0)
            acc = acc + w_ref[2 * p + 1:2 * p + 2, :] * jnp.maximum(d[:, LANES:], 0.0)
        key = _sort_key(acc)
        if masked:
            kpos = r0 + lax.broadcasted_iota(I32, (ATT_KC, LANES), 0)
            qpos = i * ATT_QB + lax.broadcasted_iota(I32, (ATT_KC, LANES), 1)
            key = jnp.where(kpos <= qpos, key, INT_MIN)
        st_ref[pl.ds(r0, ATT_KC), :] = key

    def full_chunk(c, carry):
        score_chunk(c, False)
        return carry

    lax.fori_loop(0, n_chunks - 1, full_chunk, 0)
    score_chunk(n_chunks - 1, True)

    thr = jnp.maximum(_kth_largest_key(st_ref, n_chunks * (ATT_KC // LANES), topk), INT_MIN + 1)

    m_ref[...] = jnp.full_like(m_ref, NEG_INF)
    l_ref[...] = jnp.zeros_like(l_ref)
    acc_ref[...] = jnp.zeros_like(acc_ref)

    def attend_chunk(c, carry):
        r0 = pl.multiple_of(c * ATT_KC, ATT_KC)
        bias = jnp.where(st_ref[pl.ds(r0, ATT_KC), :] >= thr, 0.0, NEG_INF)
        for p in range(N_HEADS // 2):
            d = _dot(k_ref[pl.ds(r0, ATT_KC), LANES * p:LANES * (p + 1)], qbd_ref[p])
            for half in range(2):
                h = 2 * p + half
                s = d[:, LANES * half:LANES * (half + 1)] + bias
                m_old = m_ref[h:h + 1, :]
                m_new = jnp.maximum(m_old, jnp.max(s, axis=0, keepdims=True))
                m_safe = jnp.where(m_new == NEG_INF, 0.0, m_new)
                pr = jnp.exp(s - m_safe)
                alpha = jnp.exp(m_old - m_safe)
                l_ref[h:h + 1, :] = alpha * l_ref[h:h + 1, :] + jnp.sum(pr, axis=0, keepdims=True)
                rows = slice(HEAD_DIM * h, HEAD_DIM * (h + 1))
                pv = _dot(vt_ref[rows, pl.ds(r0, ATT_KC)], pr.astype(BF16))
                acc_ref[rows, :] = alpha * acc_ref[rows, :] + pv
                m_ref[h:h + 1, :] = m_new
        return carry

    lax.fori_loop(0, n_chunks, attend_chunk, 0)
    out_t = jnp.concatenate(
        [acc_ref[HEAD_DIM * h:HEAD_DIM * (h + 1), :] / l_ref[h:h + 1, :] for h in range(N_HEADS)], axis=0)
    o_ref[...] = out_t.T


def _attn_prompt(qi, wb, q, ki4, kbf, vt, topk):
    s = q.shape[0]
    assert s % ATT_KC == 0
    row = lambda n: pl.BlockSpec((ATT_QB, n), lambda i: (i, 0))
    resident = pl.BlockSpec(memory_space=pltpu.VMEM)
    vm = lambda shape, dt: pltpu.VMEM(shape, dt)
    return pl.pallas_call(
        functools.partial(_attn_prompt_body, topk=topk),
        grid=(s // ATT_QB,),
        in_specs=[row(1024), row(LANES), row(ATT_WIDTH), resident, resident, resident],
        out_specs=row(ATT_WIDTH),
        out_shape=jax.ShapeDtypeStruct((s, ATT_WIDTH), F32),
        scratch_shapes=[vm((s, LANES), I32), vm((IDX_HEADS // 2, 2 * LANES, 2 * LANES), BF16),
                        vm((N_HEADS // 2, LANES, 2 * LANES), BF16), vm((IDX_HEADS, LANES), F32),
                        vm((ATT_WIDTH, LANES), F32), vm((N_HEADS, LANES), F32), vm((N_HEADS, LANES), F32)],
        compiler_params=_params(("arbitrary",), 60 * 1024 * 1024),
        name="attn_prompt",
    )(qi, wb, q, ki4, kbf, vt)


SEL_SLOTS = 272
PAGE_GROUP = 8


def _sample_scores_body(pt_ref, qi_ref, w_ref, kin_ref, ik_hbm, sc_ref, buf_ref, sem_ref, *, layer, n_pages):
    b = pl.program_id(0)
    nb = pl.num_programs(0)

    def page_copy(seq, p, slot):
        return pltpu.make_async_copy(ik_hbm.at[layer, pt_ref[seq, p]], buf_ref.at[slot, p], sem_ref.at[slot])

    def issue(seq, slot):
        for p in range(n_pages):
            page_copy(seq, p, slot).start()

    @pl.when(b == 0)
    def _():
        issue(0, 0)

    @pl.when(b + 1 < nb)
    def _():
        issue(b + 1, (b + 1) % 2)

    slot = b % 2
    for p in range(n_pages):
        page_copy(b, p, slot).wait()

    qi = qi_ref[0]
    qh, ql = _split(qi)
    w = w_ref[0] * (IDX_DIM ** -0.5 * IDX_HEADS ** -0.5)
    rows = PAGE_GROUP * PAGE_SIZE
    for g in range(n_pages // PAGE_GROUP):
        kf = buf_ref[slot, PAGE_GROUP * g:PAGE_GROUP * (g + 1)].reshape(rows, IDX_DIM)
        kh, kl = _split(kf)
        d = _dot_nt(qh, kh) + _dot_nt(ql, kh) + _dot_nt(qh, kl)
        sc_ref[0, :, rows * g:rows * (g + 1)] = jnp.sum(w * jnp.maximum(d, 0.0), axis=0, keepdims=True)
    dn = jnp.sum(qi * kin_ref[0], axis=1, keepdims=True)
    sn = jnp.sum(w * jnp.maximum(dn, 0.0), axis=0, keepdims=True)
    lane = lax.broadcasted_iota(I32, (1, LANES), 1)
    sc_ref[0, :, n_pages * PAGE_SIZE:] = jnp.where(lane == 0, sn, NEG_INF)


def _sample_scores(page_table, qi, wi, ki_new, cache_ik, layer):
    nb, n_pages = page_table.shape
    assert n_pages % PAGE_GROUP == 0
    width = n_pages * PAGE_SIZE + LANES
    blk = lambda a: pl.BlockSpec((1,) + a.shape[1:], lambda b, pt: (b, 0, 0))
    qi3 = qi.reshape(nb, IDX_HEADS, IDX_DIM)
    w3 = wi.reshape(nb, IDX_HEADS, 1)
    kin3 = ki_new.reshape(nb, 1, IDX_DIM)
    return pl.pallas_call(
        functools.partial(_sample_scores_body, layer=layer, n_pages=n_pages),
        grid_spec=pltpu.PrefetchScalarGridSpec(
            num_scalar_prefetch=1,
            grid=(nb,),
            in_specs=[blk(qi3), blk(w3), blk(kin3), pl.BlockSpec(memory_space=pl.ANY)],
            out_specs=pl.BlockSpec((1, 1, width), lambda b, pt: (b, 0, 0)),
            scratch_shapes=[pltpu.VMEM((2, n_pages, PAGE_SIZE, IDX_DIM), F32), pltpu.SemaphoreType.DMA((2,))],
        ),
        out_shape=jax.ShapeDtypeStruct((nb, 1, width), F32),
        compiler_params=_params(("arbitrary",)),
        name="sample_scores",
    )(page_table, qi3, w3, kin3, cache_ik)


def _sample_select_body(sc_ref, pos_ref, cnt_ref, st_ref, pre_ref, *, topk):
    n = sc_ref.shape[0]
    zero = pl.program_id(0)
    ntiles = n // LANES + zero

    def to_keys(t, carry):
        rows = pl.ds(pl.multiple_of(t * LANES, LANES), LANES)
        x = sc_ref[rows, :]
        st_ref[rows, :] = jnp.where(x == NEG_INF, INT_MIN, _sort_key(x))
        return carry

    lax.fori_loop(0, ntiles, to_keys, 0)
    thr = jnp.maximum(_kth_largest_key(st_ref, ntiles, topk), INT_MIN + 1)
    thr_b = jnp.broadcast_to(thr, (SUBLANES, LANES))
    row = lax.broadcasted_iota(I32, (SUBLANES, LANES), 0)

    def prefix(t, carry):
        rows = pl.ds(pl.multiple_of(t * SUBLANES, SUBLANES), SUBLANES)
        x = (st_ref[rows, :] >= thr_b).astype(I32)
        for sh in (1, 2, 4):
            x = x + jnp.where(row >= sh, pltpu.roll(x, sh, 0), 0)
        x = x + carry
        pre_ref[rows, :] = x
        return jnp.broadcast_to(x[SUBLANES - 1:SUBLANES, :], (SUBLANES, LANES))

    total = lax.fori_loop(0, n // SUBLANES + zero, prefix, jnp.zeros((SUBLANES, LANES), I32))
    cnt_ref[...] = total[0:1, :]

    def slot(r, carry):
        def body(t, accs):
            base = pl.multiple_of(t * LANES, LANES)
            accs = list(accs)
            for k in range(LANES // SUBLANES):
                v = pre_ref[pl.ds(base + k * SUBLANES, SUBLANES), :]
                accs[k % 4] = accs[k % 4] + (v <= r).astype(I32)
            return tuple(accs)

        accs = lax.fori_loop(0, ntiles, body, (jnp.zeros((SUBLANES, LANES), I32),) * 4)
        pos_ref[pl.ds(r, 1), :] = jnp.sum(accs[0] + accs[1] + accs[2] + accs[3], axis=0, keepdims=True)
        return carry

    lax.fori_loop(0, SEL_SLOTS + zero, slot, 0)


def _sample_select(sc_t, topk):
    n, nb = sc_t.shape
    assert nb == LANES and n % LANES == 0
    whole = lambda shape: pl.BlockSpec(shape, lambda i: (0, 0))
    return pl.pallas_call(
        functools.partial(_sample_select_body, topk=topk),
        grid=(1,),
        in_specs=[whole((n, nb))],
        out_specs=[whole((SEL_SLOTS, nb)), whole((1, nb))],
        out_shape=[jax.ShapeDtypeStruct((SEL_SLOTS, nb), I32), jax.ShapeDtypeStruct((1, nb), I32)],
        scratch_shapes=[pltpu.VMEM((n, nb), I32), pltpu.VMEM((n, nb), I32)],
        compiler_params=_params(("arbitrary",)),
        name="sample_select",
    )(sc_t)


def _sample_attend_body(pt_ref, pos_ref, cnt_ref, q_ref, kn_ref, vn_ref, ck_hbm, cv_hbm, o_ref,
                        kg_ref, vg_ref, sem_ref, *, layer, past):
    b = pl.program_id(0)
    nb = pl.num_programs(0)

    def row_copies(seq, r, slot):
        pos = jnp.minimum(pos_ref[seq, r], past - 1)
        page = pt_ref[seq, pos // PAGE_SIZE]
        off = pos % PAGE_SIZE
        return (pltpu.make_async_copy(ck_hbm.at[layer, page, off], kg_ref.at[slot, r], sem_ref.at[slot]),
                pltpu.make_async_copy(cv_hbm.at[layer, page, off], vg_ref.at[slot, r], sem_ref.at[slot]))

    def issue(seq, slot):
        def body(r, carry):
            for c in row_copies(seq, r, slot):
                c.start()
            return carry
        lax.fori_loop(0, SEL_SLOTS, body, 0)

    @pl.when(b == 0)
    def _():
        issue(0, 0)

    @pl.when(b + 1 < nb)
    def _():
        issue(b + 1, (b + 1) % 2)

    slot = b % 2

    def wait_body(r, carry):
        for c in row_copies(b, r, slot):
            c.wait()
        return carry

    lax.fori_loop(0, SEL_SLOTS, wait_body, 0)

    n_sel = jnp.minimum(cnt_ref[b], SEL_SLOTS)
    last = jnp.maximum(n_sel - 1, 0)
    has_new = pos_ref[b, last] == past
    r_idx = lax.broadcasted_iota(I32, (SEL_SLOTS, N_HEADS, 1), 0)
    is_new = (r_idx == last) & has_new
    valid = r_idx < n_sel
    q = q_ref[...]
    kg = jnp.where(is_new, kn_ref[...], kg_ref[slot])
    vg = jnp.where(is_new, vn_ref[...], vg_ref[slot])
    s = jnp.sum(kg * q, axis=-1, keepdims=True) * HEAD_DIM ** -0.5
    s = jnp.where(valid, s, NEG_INF)
    pr = jnp.exp(s - jnp.max(s, axis=0, keepdims=True))
    out = jnp.sum(pr * vg, axis=0, keepdims=True) / jnp.sum(pr, axis=0, keepdims=True)
    o_ref[...] = out


def _sample_attend(page_table, pos, cnt, q, k_new, v_new, cache_k, cache_v, layer):
    nb, n_pages = page_table.shape
    q3, kn3, vn3 = (a.reshape(nb, N_HEADS, HEAD_DIM) for a in (q, k_new, v_new))
    blk = pl.BlockSpec((1, N_HEADS, HEAD_DIM), lambda b, *_: (b, 0, 0))
    hbm = pl.BlockSpec(memory_space=pl.ANY)
    out = pl.pallas_call(
        functools.partial(_sample_attend_body, layer=layer, past=n_pages * PAGE_SIZE),
        grid_spec=pltpu.PrefetchScalarGridSpec(
            num_scalar_prefetch=3,
            grid=(nb,),
            in_specs=[blk, blk, blk, hbm, hbm],
            out_specs=blk,
            scratch_shapes=[pltpu.VMEM((2, SEL_SLOTS, N_HEADS, HEAD_DIM), F32),
                            pltpu.VMEM((2, SEL_SLOTS, N_HEADS, HEAD_DIM), F32),
                            pltpu.SemaphoreType.DMA((2,))],
        ),
        out_shape=jax.ShapeDtypeStruct((nb, N_HEADS, HEAD_DIM), F32),
        compiler_params=_params(("arbitrary",)),
        name="sample_attend",
    )(page_table, pos, cnt, q3, kn3, vn3, cache_k, cache_v)
    return out.reshape(nb, ATT_WIDTH)


def _attn_sample(z, page_table, cache_k, cache_v, cache_ik, layer, topk):
    nb = page_table.shape[0]
    ki_new = z["ka"][:, :IDX_DIM]
    wi = z["wb"][:, :IDX_HEADS]
    sc = _sample_scores(page_table, z["qi"], wi, ki_new, cache_ik, layer)
    pos_t, cnt = _sample_select(sc.reshape(nb, -1).T, topk)
    return _sample_attend(page_table, pos_t.T, cnt.reshape(nb), z["q"], z["k"], z["v"], cache_k, cache_v, layer)


S5_SUB = 64
S5_CHUNK = SUBLANES * S5_SUB
S5_STRIP = 512


def _s5_prep_body(are_ref, aim_ref, ldt_ref, bre_ref, bim_ref, cre_ref, cim_ref,
                  bbh_ref, bbl_ref, cc_ref, abar_ref, abarl_ref):
    ar = are_ref[...]
    ai = aim_ref[...]
    dt = jnp.exp(ldt_ref[...])
    mag = jnp.exp(dt * ar)
    abr = mag * jnp.cos(dt * ai)
    abi = mag * jnp.sin(dt * ai)
    den = ar * ar + ai * ai
    xr = abr - 1.0
    fre = (xr * ar + abi * ai) / den
    fim = (abi * ar - xr * ai) / den
    bre = bre_ref[...]
    bim = bim_ref[...]
    bb_re = fre * bre - fim * bim
    bb_im = fre * bim + fim * bre
    h, lo = _split(bb_re)
    bbh_ref[:, 0:SSM_FLAT] = h
    bbl_ref[:, 0:SSM_FLAT] = lo
    h, lo = _split(bb_im)
    bbh_ref[:, SSM_FLAT:] = h
    bbl_ref[:, SSM_FLAT:] = lo
    cc_ref[0:SSM_FLAT, :] = cre_ref[...].astype(BF16)
    cc_ref[SSM_FLAT:, :] = (-cim_ref[...]).astype(BF16)
    abar_ref[0:1, :] = abr
    abar_ref[1:2, :] = abi
    pr, pi = abr, abi
    for _ in range(int(math.log2(S5_SUB))):
        pr, pi = pr * pr - pi * pi, 2.0 * pr * pi
    abarl_ref[0:1, :] = pr
    abarl_ref[1:2, :] = pi


def _s5_prep(a_re, a_im, log_dt, b_re, b_im, c_re, c_im):
    eye = jnp.eye(SSM_GROUPS, dtype=F32)
    bd_b = lambda b: jnp.einsum("gpc,gh->gchp", b, eye).reshape(SSM_WIDTH, SSM_FLAT)
    bd_c = lambda c: jnp.einsum("gcp,gh->gphc", c, eye).reshape(SSM_FLAT, SSM_WIDTH)
    row = lambda a: a.reshape(1, SSM_FLAT)
    ldt = jnp.repeat(log_dt, SSM_STATE).reshape(1, SSM_FLAT)
    args = (row(a_re), row(a_im), ldt, bd_b(b_re), bd_b(b_im), bd_c(c_re), bd_c(c_im))
    return pl.pallas_call(
        _s5_prep_body,
        out_shape=[jax.ShapeDtypeStruct((SSM_WIDTH, 2 * SSM_FLAT), BF16),
                   jax.ShapeDtypeStruct((SSM_WIDTH, 2 * SSM_FLAT), BF16),
                   jax.ShapeDtypeStruct((2 * SSM_FLAT, SSM_WIDTH), BF16),
                   jax.ShapeDtypeStruct((2, SSM_FLAT), F32),
                   jax.ShapeDtypeStruct((2, SSM_FLAT), F32)],
        compiler_params=_params(None),
        name="s5_prep",
    )(*args)


def _s5_scan_body(u_ref, bbh_ref, cc_ref, abar_ref, abarl_ref, d_ref, y_ref, st_ref,
                  buf_ref, ends_ref, carry_ref):
    @pl.when(pl.program_id(0) == 0)
    def _():
        carry_ref[...] = jnp.zeros_like(carry_ref)

    u = u_ref[...]
    buf_ref[...] = _dot(u.astype(BF16), bbh_ref[...])

    def sweep(store):
        for s in range(SSM_FLAT // S5_STRIP):
            re = slice(s * S5_STRIP, (s + 1) * S5_STRIP)
            im = slice(SSM_FLAT + s * S5_STRIP, SSM_FLAT + (s + 1) * S5_STRIP)
            ar = jnp.broadcast_to(abar_ref[0:1, re], (SUBLANES, S5_STRIP))
            ai = jnp.broadcast_to(abar_ref[1:2, re], (SUBLANES, S5_STRIP))

            def step(i, c):
                sr, si = c
                rows = pl.ds(pl.multiple_of(i * SUBLANES, SUBLANES), SUBLANES)
                nr = ar * sr - ai * si + buf_ref[rows, re]
                ni = ar * si + ai * sr + buf_ref[rows, im]
                if store:
                    buf_ref[rows, re] = nr
                    buf_ref[rows, im] = ni
                return nr, ni

            if store:
                init = (ends_ref[:, re], ends_ref[:, im])
            else:
                init = (jnp.zeros((SUBLANES, S5_STRIP), F32),) * 2
            sr, si = lax.fori_loop(0, S5_SUB, step, init, unroll=4)
            if not store:
                ends_ref[:, re] = sr
                ends_ref[:, im] = si

    sweep(False)
    pr = abarl_ref[0:1, :]
    pi = abarl_ref[1:2, :]
    cr = carry_ref[0:1, 0:SSM_FLAT]
    ci = carry_ref[0:1, SSM_FLAT:]
    for j in range(SUBLANES):
        er = ends_ref[j:j + 1, 0:SSM_FLAT]
        ei = ends_ref[j:j + 1, SSM_FLAT:]
        ends_ref[j:j + 1, 0:SSM_FLAT] = cr
        ends_ref[j:j + 1, SSM_FLAT:] = ci
        cr, ci = pr * cr - pi * ci + er, pr * ci + pi * cr + ei
    carry_ref[0:1, 0:SSM_FLAT] = cr
    carry_ref[0:1, SSM_FLAT:] = ci
    sweep(True)
    y_ref[...] = _dot(buf_ref[...].astype(BF16), cc_ref[...]) + d_ref[...] * u
    st_ref[...] = carry_ref[0:1, :]


def _s5_prompt(us_perm, bbh, cc, abar, abarl, d_row):
    t = us_perm.shape[0]
    full = lambda a: pl.BlockSpec(a.shape, lambda i: (0,) * a.ndim)
    return pl.pallas_call(
        _s5_scan_body,
        grid=(t // S5_CHUNK,),
        in_specs=[pl.BlockSpec((S5_CHUNK, SSM_WIDTH), lambda i: (i, 0)),
                  full(bbh), full(cc), full(abar), full(abarl), full(d_row)],
        out_specs=[pl.BlockSpec((S5_CHUNK, SSM_WIDTH), lambda i: (i, 0)),
                   pl.BlockSpec((1, 2 * SSM_FLAT), lambda i: (0, 0))],
        out_shape=[jax.ShapeDtypeStruct((t, SSM_WIDTH), F32),
                   jax.ShapeDtypeStruct((1, 2 * SSM_FLAT), F32)],
        scratch_shapes=[pltpu.VMEM((S5_CHUNK, 2 * SSM_FLAT), F32),
                        pltpu.VMEM((SUBLANES, 2 * SSM_FLAT), F32),
                        pltpu.VMEM((SUBLANES, 2 * SSM_FLAT), F32)],
        compiler_params=_params(("arbitrary",)),
        name="s5_scan",
    )(us_perm, bbh, cc, abar, abarl, d_row)


def _s5_step_body(u_ref, hre_ref, him_ref, bbh_ref, bbl_ref, cc_ref, abar_ref, d_ref,
                  y_ref, ore_ref, oim_ref):
    u = u_ref[...]
    uh, ul = _split(u)
    bu = _dot(uh, bbh_ref[...]) + _dot(ul, bbh_ref[...]) + _dot(uh, bbl_ref[...])
    ar = abar_ref[0:1, :]
    ai = abar_ref[1:2, :]
    hr = hre_ref[...]
    hi = him_ref[...]
    nr = ar * hr - ai * hi + bu[:, 0:SSM_FLAT]
    ni = ar * hi + ai * hr + bu[:, SSM_FLAT:]
    ore_ref[...] = nr
    oim_ref[...] = ni
    hcat = jnp.concatenate([nr, ni], axis=1).astype(BF16)
    y_ref[...] = _dot(hcat, cc_ref[...]) + d_ref[...] * u


def _s5_sample(us, h_re, h_im, bbh, bbl, cc, abar, d_row):
    n = us.shape[0]
    return pl.pallas_call(
        _s5_step_body,
        out_shape=[jax.ShapeDtypeStruct((n, SSM_WIDTH), F32),
                   jax.ShapeDtypeStruct((n, SSM_FLAT), F32),
                   jax.ShapeDtypeStruct((n, SSM_FLAT), F32)],
        compiler_params=_params(None),
        name="s5_step",
    )(us, h_re, h_im, bbh, bbl, cc, abar, d_row)


def _s5_permute(x, inverse=False):
    t, w = x.shape
    a, b = (S5_SUB, SUBLANES) if inverse else (SUBLANES, S5_SUB)
    return x.reshape(t // S5_CHUNK, a, b, w).transpose(0, 2, 1, 3).reshape(t, w)


def _mix_body(att_ref, y_ref, ga_ref, gb_ref, h_ref, watt_ref, wglu_ref, wout_ref, gffn_ref,
              h1_ref, unh_ref, unl_ref):
    branch_a = _dot(att_ref[...].astype(BF16), watt_ref[...])
    zs = _gelu(y_ref[...]).astype(BF16)
    glu = _dot(zs, wglu_ref[...])
    branch_b = glu[:, 0:D_MODEL] * _sigmoid(glu[:, D_MODEL:])
    mixed = _sigmoid(ga_ref[...]) * branch_a + _sigmoid(gb_ref[...]) * branch_b
    h1 = h_ref[...] + _dot(mixed.astype(BF16), wout_ref[...])
    h1_ref[...] = h1
    un_t = _rmsnorm(h1, gffn_ref[...]).T
    hi, lo = _split(un_t)
    unh_ref[...] = hi
    unl_ref[...] = lo


def _mix(att, y, ga, gb, h, watt, wglu, wout, gffn, tb):
    t = h.shape[0]
    row = lambda n: pl.BlockSpec((tb, n), lambda i: (i, 0))
    full = lambda a: pl.BlockSpec(a.shape, lambda i: (0,) * a.ndim)
    col = pl.BlockSpec((D_MODEL, tb), lambda i: (0, i))
    return pl.pallas_call(
        _mix_body,
        grid=(t // tb,),
        in_specs=[row(512), row(512), row(1024), row(1024), row(1024),
                  full(watt), full(wglu), full(wout), full(gffn)],
        out_specs=[row(1024), col, col],
        out_shape=[jax.ShapeDtypeStruct((t, D_MODEL), F32),
                   jax.ShapeDtypeStruct((D_MODEL, t), BF16),
                   jax.ShapeDtypeStruct((D_MODEL, t), BF16)],
        compiler_params=_params(("arbitrary",)),
        name="mix",
    )(att, y, ga, gb, h, watt, wglu, wout, gffn)


PEER_ET = 1024
PEER_I0 = PEER_ET // PEER_NKEYS
PEER_HJ = PEER_HEADS * 2


def _top_extract(x, iota, rounds, emit):
    n = x.shape[0]
    for r in range(rounds):
        m = jnp.max(x, axis=0, keepdims=True)
        emit(r, m)
        if r + 1 < rounds:
            first = jnp.min(jnp.where(x == m, iota, n), axis=0, keepdims=True)
            x = jnp.where(iota == first, NEG_INF, x)


def _peer_body(unh_ref, unl_ref, h1_ref, wqh_ref, wql_ref, keys4_ref, u_ref, vt_ref, o_ref,
               s_ref, e_ref, sv_ref, thr_ref, rz_ref, a_ref, g_ref, acc_ref, *, tb):
    j = pl.program_id(1)
    nlt = tb // LANES

    def lane_tile(lt):
        return pl.ds(pl.multiple_of(lt * LANES, LANES), LANES)

    @pl.when(j == 0)
    def _():
        unh = unh_ref[...]
        wqh = wqh_ref[...]
        qt = _dot(wqh, unh) + _dot(wqh, unl_ref[...]) + _dot(wql_ref[...], unh)
        for hj in range(PEER_HJ):
            xh, xl = _split(qt[PEER_HALF * hj:PEER_HALF * (hj + 1)])
            s_ref[hj] = _dot(keys4_ref[hj % 2], jnp.concatenate([xh, xl, xh, xl], axis=0))

        iota1 = lax.broadcasted_iota(I32, (PEER_NKEYS, LANES), 0)

        def stage1(idx, carry):
            hj, lt = idx // nlt, idx % nlt
            lanes = lane_tile(lt)
            x = s_ref[hj, :, lanes]

            def emit(r, m):
                sv_ref[hj, r:r + 1, lanes] = m

            _top_extract(x, iota1, PEER_TOPK, emit)
            e_ref[hj, :, lanes] = jnp.exp(x - sv_ref[hj, 0:1, lanes])
            return carry

        lax.fori_loop(0, PEER_HJ * nlt, stage1, 0)

        iota2 = lax.broadcasted_iota(I32, (PEER_TOPK * PEER_TOPK, LANES), 0)

        def stage2(idx, carry):
            h, lt = idx // nlt, idx % nlt
            lanes = lane_tile(lt)
            sv0 = sv_ref[2 * h, :, lanes]
            sv1 = sv_ref[2 * h + 1, :, lanes]
            cand = jnp.concatenate([sv0[a:a + 1] + sv1 for a in range(PEER_TOPK)], axis=0)
            top = []
            _top_extract(cand, iota2, PEER_TOPK, lambda r, m: top.append(m))
            z = sum(jnp.exp(c - top[0]) for c in top)
            thr_ref[h, :, lanes] = top[-1]
            rz_ref[h, :, lanes] = 1.0 / z
            return carry

        lax.fori_loop(0, PEER_HEADS * nlt, stage2, 0)
        acc_ref[...] = jnp.zeros_like(acc_ref)

    a_ref[...] = _dot(u_ref[...], unh_ref[...])

    def dense(lt, carry):
        lanes = lane_tile(lt)
        grp = pl.ds(pl.multiple_of(j * PEER_I0, SUBLANES), PEER_I0)
        s0g = [s_ref[2 * h, grp, lanes] for h in range(PEER_HEADS)]
        e0g = [e_ref[2 * h, grp, lanes] * rz_ref[h, :, lanes] for h in range(PEER_HEADS)]
        for r in range(PEER_I0):
            w = jnp.zeros((PEER_NKEYS, LANES), F32)
            for h in range(PEER_HEADS):
                sel = (s_ref[2 * h + 1, :, lanes] + s0g[h][r:r + 1]) >= thr_ref[h, :, lanes]
                w = w + jnp.where(sel, e_ref[2 * h + 1, :, lanes] * e0g[h][r:r + 1], 0.0)
            rows = slice(r * PEER_NKEYS, (r + 1) * PEER_NKEYS)
            g_ref[rows, lanes] = (_gelu(a_ref[rows, lanes]) * w).astype(BF16)
        return carry

    lax.fori_loop(0, nlt, dense, 0)
    acc_ref[...] += _dot(vt_ref[...], g_ref[...])

    @pl.when(j == pl.num_programs(1) - 1)
    def _():
        o_ref[...] = h1_ref[...] + acc_ref[...].T


def _peer(unh, unl, h1, wqh, wql, keys4, u_bf, vt_bf, tb):
    t = h1.shape[0]
    col = pl.BlockSpec((D_MODEL, tb), lambda i, j: (0, i))
    full = lambda a: pl.BlockSpec(a.shape, lambda i, j: (0,) * a.ndim)
    vm = lambda shape, dt: pltpu.VMEM(shape, dt)
    return pl.pallas_call(
        functools.partial(_peer_body, tb=tb),
        grid=(t // tb, PEER_EXPERTS // PEER_ET),
        in_specs=[col, col, pl.BlockSpec((tb, D_MODEL), lambda i, j: (i, 0)),
                  full(wqh), full(wql), full(keys4),
                  pl.BlockSpec((PEER_ET, D_MODEL), lambda i, j: (j, 0)),
                  pl.BlockSpec((D_MODEL, PEER_ET), lambda i, j: (0, j))],
        out_specs=pl.BlockSpec((tb, D_MODEL), lambda i, j: (i, 0)),
        out_shape=jax.ShapeDtypeStruct((t, D_MODEL), F32),
        scratch_shapes=[vm((PEER_HJ, PEER_NKEYS, tb), F32), vm((PEER_HJ, PEER_NKEYS, tb), F32),
                        vm((PEER_HJ, PEER_TOPK, tb), F32), vm((PEER_HEADS, 1, tb), F32),
                        vm((PEER_HEADS, 1, tb), F32), vm((PEER_ET, tb), F32), vm((PEER_ET, tb), BF16),
                        vm((D_MODEL, tb), F32)],
        compiler_params=_params(("arbitrary", "arbitrary")),
        name="peer",
    )(unh, unl, h1, wqh, wql, keys4, u_bf, vt_bf)


def _prep_peer_weights(peer_wq, peer_keys, peer_u, peer_v):
    wqt = peer_wq.T
    wqh = wqt.astype(BF16)
    wql = (wqt - wqh.astype(F32)).astype(BF16)
    kh = peer_keys.astype(BF16)
    kl = (peer_keys - kh.astype(F32)).astype(BF16)
    keys4 = jnp.concatenate([kh, kh, kl, kl], axis=-1)
    return wqh, wql, keys4, peer_u.astype(BF16), peer_v.T.astype(BF16)


def _ple_body(h_ref, p_ref, gple_ref, wgate_ref, wproj_ref, gfin_ref, y_ref):
    h = h_ref[...]
    gate = _sigmoid(_dot(_rmsnorm(h, gple_ref[...]).astype(BF16), wgate_ref[...]))
    h = h + gate * _dot(p_ref[...].astype(BF16), wproj_ref[...])
    y_ref[...] = _rmsnorm(h, gfin_ref[...])


def _ple(h, p, gple, wgate, wproj, gfin, tb):
    t = h.shape[0]
    row = lambda n: pl.BlockSpec((tb, n), lambda i: (i, 0))
    full = lambda a: pl.BlockSpec(a.shape, lambda i: (0,) * a.ndim)
    return pl.pallas_call(
        _ple_body,
        grid=(t // tb,),
        in_specs=[row(D_MODEL), row(p.shape[1]), full(gple), full(wgate), full(wproj), full(gfin)],
        out_specs=row(D_MODEL),
        out_shape=jax.ShapeDtypeStruct((t, D_MODEL), F32),
        compiler_params=_params(("arbitrary",)),
        name="ple_final",
    )(h, p, gple, wgate, wproj, gfin)


def kernel(x_prompt, x_sample, p_prompt, p_sample, cache_k, cache_v, cache_idx_k, state_ssm_re, state_ssm_im, page_table, norm_mix, w_in, ssm_a_re, ssm_a_im, ssm_log_dt, ssm_b_re, ssm_b_im, ssm_c_re, ssm_c_im, ssm_d, w_glu, w_att_br, w_out, norm_ffn, peer_wq, peer_keys, peer_u, peer_v, norm_ple, ple_gate, ple_proj, norm_final):
    l = 0
    whi, wlo, wfast = _prep_in_weights(w_in[l])
    g = norm_mix[l][None, :]
    zp = _in_projection(x_prompt[0], g, whi, wlo, wfast, 256)
    zs = _in_projection(x_sample[:, 0], g, whi, wlo, wfast, 128)
    s5p, s5s = _s5_all(zp["us"], zs["us"], state_ssm_re[l], state_ssm_im[l], ssm_a_re[l], ssm_a_im[l],
                       ssm_log_dt[l], ssm_b_re[l], ssm_b_im[l], ssm_c_re[l], ssm_c_im[l], ssm_d[l])
    seq = x_prompt.shape[1]
    att_p = _attn_prompt(zp["qi"], zp["wb"], zp["q"], zp["ki4"], zp["kbf"], zp["vt"], min(TOPK_MAX, seq // 4))
    watt, wglu, wout = (w.astype(BF16) for w in (w_att_br[l], w_glu[l], w_out[l]))
    pw = _prep_peer_weights(peer_wq[l], peer_keys[l], peer_u[l], peer_v[l])
    wgate, wproj = ple_gate[l].astype(BF16), ple_proj[l].astype(BF16)

    def tail(att, y_ssm, z, h, p, tb):
        h1, unh, unl = _mix(att, y_ssm, z["ga"], z["gb"], h, watt, wglu, wout, norm_ffn[l][None, :], tb)
        h2 = _peer(unh, unl, h1, *pw, tb)
        return _ple(h2, p, norm_ple[l][None, :], wgate, wproj, norm_final[None, :], tb)

    y_p = tail(att_p, s5p[0], zp, x_prompt[0], p_prompt[l, 0], 512)
    nb = x_sample.shape[0]
    past = page_table.shape[1] * PAGE_SIZE
    att_s = _attn_sample(zs, page_table, cache_k, cache_v, cache_idx_k, l, min(TOPK_MAX, (past + 1) // 4))
    y_s = tail(att_s, s5s[0], zs, x_sample[:, 0], p_sample[l, :, 0], 128)

    heads = lambda a, lead: a.reshape(lead + (N_HEADS, HEAD_DIM))
    st = lambda a, lead: a.reshape(lead + (SSM_GROUPS, SSM_STATE))
    return (y_p[None], y_s[:, None],
            heads(zp["k"], (1, 1, seq)), heads(zp["v"], (1, 1, seq)), zp["ka"][:, :IDX_DIM].reshape(1, 1, seq, IDX_DIM),
            heads(zs["k"], (1, nb, 1)), heads(zs["v"], (1, nb, 1)), zs["ka"][:, :IDX_DIM].reshape(1, nb, 1, IDX_DIM),
            st(s5p[1], (1, 1)), st(s5p[2], (1, 1)), st(s5s[1], (1, nb)), st(s5s[2], (1, nb)))


def _s5_all(us_p, us_s, h_re, h_im, a_re, a_im, log_dt, b_re, b_im, c_re, c_im, d):
    bbh, bbl, cc, abar, abarl = _s5_prep(a_re, a_im, log_dt, b_re, b_im, c_re, c_im)
    d_row = d.reshape(1, SSM_WIDTH)
    y_perm, st = _s5_prompt(_s5_permute(us_p), bbh, cc, abar, abarl, d_row)
    y_p = _s5_permute(y_perm, inverse=True)
    n = us_s.shape[0]
    y_s, sre, sim = _s5_sample(us_s, h_re.reshape(n, SSM_FLAT), h_im.reshape(n, SSM_FLAT),
                               bbh, bbl, cc, abar, d_row)
    return (y_p, st[:, :SSM_FLAT], st[:, SSM_FLAT:]), (y_s, sre, sim)
```

```python
import functools
import math

import jax
import jax.numpy as jnp
import numpy as np
from jax import lax
from jax.experimental import pallas as pl
from jax.experimental.pallas import tpu as pltpu

F32 = jnp.float32
BF16 = jnp.bfloat16
I32 = jnp.int32

D_MODEL = 1024
PAGE_SIZE = 128
N_HEADS = 8
HEAD_DIM = 64
ATT_WIDTH = N_HEADS * HEAD_DIM
IDX_HEADS = 16
IDX_DIM = 64
TOPK_MAX = 256
SSM_GROUP = 16
SSM_WIDTH = 512
SSM_GROUPS = SSM_WIDTH // SSM_GROUP
SSM_STATE = 64
SSM_FLAT = SSM_GROUPS * SSM_STATE
PEER_HEADS = 8
PEER_NKEYS = 128
PEER_EXPERTS = PEER_NKEYS * PEER_NKEYS
PEER_HALF = 64
PEER_TOPK = 16
EPS = 1e-6

LANES = 128
SUBLANES = 8
VMEM_LIMIT = 56 * 1024 * 1024
INT_MIN = -(2 ** 31)
NEG_INF = float("-inf")


def _params(sem, vmem=VMEM_LIMIT):
    return pltpu.CompilerParams(dimension_semantics=sem, vmem_limit_bytes=vmem)


def _split(x):
    hi = x.astype(BF16)
    lo = (x - hi.astype(F32)).astype(BF16)
    return hi, lo


def _dot(a, b):
    return jnp.dot(a, b, preferred_element_type=F32)


def _dot_nt(a, b):
    return lax.dot_general(a, b, (((1,), (1,)), ((), ())), preferred_element_type=F32)


def _rmsnorm(x, g):
    return x * lax.rsqrt(jnp.mean(x * x, axis=-1, keepdims=True) + EPS) * g


def _sigmoid(x):
    return 1.0 / (1.0 + jnp.exp(-x))


def _gelu(x):
    return 0.5 * x * (1.0 + lax.erf(x * (2.0 ** -0.5)))


def _sort_key(x):
    b = lax.bitcast_convert_type(x, I32)
    return b ^ ((b >> 31) & 0x7FFFFFFF)


_PREC_COLS = (IDX_HEADS * IDX_DIM, LANES, LANES, SSM_WIDTH)
_FAST_COLS = (ATT_WIDTH, ATT_WIDTH, ATT_WIDTH, D_MODEL, D_MODEL)


def _inproj_prec_body(x_ref, g_ref, whi_ref, wlo_ref, qi_ref, ka_ref, wb_ref, us_ref, ki4_ref):
    u = _rmsnorm(x_ref[...], g_ref[...])
    uh, ul = _split(u)

    def proj(off, w):
        wh = whi_ref[:, off:off + w]
        return _dot(uh, wh) + _dot(ul, wh) + _dot(uh, wlo_ref[:, off:off + w])

    qi_ref[...] = proj(0, 1024)
    ka = proj(1024, LANES)
    ka_ref[...] = ka
    kh, kl = _split(ka)
    ki4_ref[:, 0:LANES] = kh
    ki4_ref[:, LANES:2 * LANES] = kl
    wb_ref[...] = proj(1024 + LANES, LANES)
    us_ref[...] = proj(1024 + 2 * LANES, SSM_WIDTH)


def _inproj_fast_body(x_ref, g_ref, w_ref, q_ref, k_ref, v_ref, ga_ref, gb_ref, kbf_ref, vt_ref):
    u = _rmsnorm(x_ref[...], g_ref[...]).astype(BF16)
    q_ref[...] = _dot(u, w_ref[:, 0:512])
    k = _dot(u, w_ref[:, 512:1024])
    k_ref[...] = k
    kbf_ref[...] = k.astype(BF16)
    v = _dot(u, w_ref[:, 1024:1536])
    v_ref[...] = v
    vt_ref[...] = v.T.astype(BF16)
    ga_ref[...] = _dot(u, w_ref[:, 1536:2560])
    gb_ref[...] = _dot(u, w_ref[:, 2560:3584])


def _in_projection(x, g, w_prec_hi, w_prec_lo, w_fast, tb):
    t = x.shape[0]
    grid = (t // tb,)
    row = lambda n: pl.BlockSpec((tb, n), lambda i: (i, 0))
    full = lambda a: pl.BlockSpec(a.shape, lambda i: (0,) * a.ndim)
    qi, ka, wb, us, ki4 = pl.pallas_call(
        _inproj_prec_body,
        grid=grid,
        in_specs=[row(D_MODEL), full(g), full(w_prec_hi), full(w_prec_lo)],
        out_specs=[row(1024), row(LANES), row(LANES), row(SSM_WIDTH), row(2 * LANES)],
        out_shape=[jax.ShapeDtypeStruct((t, 1024), F32), jax.ShapeDtypeStruct((t, LANES), F32),
                   jax.ShapeDtypeStruct((t, LANES), F32), jax.ShapeDtypeStruct((t, SSM_WIDTH), F32),
                   jax.ShapeDtypeStruct((t, 2 * LANES), BF16)],
        compiler_params=_params(("arbitrary",)),
        name="inproj_prec",
    )(x, g, w_prec_hi, w_prec_lo)
    q, k, v, ga, gb, kbf, vt = pl.pallas_call(
        _inproj_fast_body,
        grid=grid,
        in_specs=[row(D_MODEL), full(g), full(w_fast)],
        out_specs=[row(512), row(512), row(512), row(1024), row(1024), row(512),
                   pl.BlockSpec((512, tb), lambda i: (0, i))],
        out_shape=[jax.ShapeDtypeStruct((t, 512), F32)] * 3 + [jax.ShapeDtypeStruct((t, 1024), F32)] * 2
        + [jax.ShapeDtypeStruct((t, 512), BF16), jax.ShapeDtypeStruct((512, t), BF16)],
        compiler_params=_params(("arbitrary",)),
        name="inproj_fast",
    )(x, g, w_fast)
    return dict(qi=qi, ka=ka, wb=wb, us=us, ki4=ki4, q=q, k=k, v=v, ga=ga, gb=gb, kbf=kbf, vt=vt)


def _prep_in_weights(w_in):
    o = np.cumsum([0, 512, 512, 512, 1024, 64, 16, 512, 1024, 1024])
    wq, wk, wv, wqi, wki, wwi, wus, wga, wgb = [w_in[:, o[j]:o[j + 1]] for j in range(9)]
    wwi_pad = jnp.pad(wwi, ((0, 0), (0, LANES - IDX_HEADS)))
    w_prec = jnp.concatenate([wqi, wki, wki, wwi_pad, wus], axis=1)
    hi = w_prec.astype(BF16)
    lo = (w_prec - hi.astype(F32)).astype(BF16)
    w_fast = jnp.concatenate([wq, wk, wv, wga, wgb], axis=1).astype(BF16)
    return hi, lo, w_fast


KTH_ROWS = 512


def _kth_largest_key(st_ref, ngroups, topk):
    def count_ge(cand):
        cb = jnp.broadcast_to(cand, (SUBLANES, LANES))

        def body(g, accs):
            base = pl.multiple_of(g * KTH_ROWS, KTH_ROWS)
            accs = list(accs)
            for r in range(KTH_ROWS // SUBLANES):
                v = st_ref[pl.ds(base + r * SUBLANES, SUBLANES), :]
                accs[r % 4] = accs[r % 4] + (v >= cb).astype(I32)
            return tuple(accs)

        accs = lax.fori_loop(0, ngroups, body, (jnp.zeros((SUBLANES, LANES), I32),) * 4)
        return jnp.sum(accs[0] + accs[1] + accs[2] + accs[3], axis=0, keepdims=True)

    def cond(c):
        b, _, cnt = c
        return jnp.logical_and(b < 32, jnp.max(cnt) > topk)

    def bit_step(c):
        b, res, cnt = c
        bit = jnp.left_shift(jnp.int32(1), 31 - b)
        cand = jnp.where(b == 0, jnp.zeros_like(res), res | bit)
        n = count_ge(cand)
        take = n >= topk
        return b + 1, jnp.where(take, cand, res), jnp.where(take, n, cnt)

    res0 = jnp.full((1, LANES), INT_MIN, I32)
    _, res, _ = lax.while_loop(cond, bit_step, (jnp.int32(0), res0, count_ge(res0 + 1)))
    return res


ATT_QB = 128
ATT_KC = 512


def _attn_prompt_body(qi_ref, wb_ref, q_ref, ki4_ref, k_ref, vt_ref, o_ref,
                      st_ref, qstat_ref, qbd_ref, w_ref, acc_ref, m_ref, l_ref, *, topk):
    i = pl.program_id(0)

    @pl.when(i == 0)
    def _():
        qbd_ref[...] = jnp.zeros_like(qbd_ref)

    qi_t = qi_ref[...].T
    for h in range(IDX_HEADS):
        xh, xl = _split(qi_t[IDX_DIM * h:IDX_DIM * (h + 1)])
        c = slice((h % 2) * LANES, (h % 2 + 1) * LANES)
        for r, piece in enumerate((xh, xl, xh, xl)):
            qstat_ref[h // 2, IDX_DIM * r:IDX_DIM * (r + 1), c] = piece
    w_ref[...] = wb_ref[...].T[0:IDX_HEADS] * (IDX_DIM ** -0.5 * IDX_HEADS ** -0.5)
    q_t = q_ref[...].T * HEAD_DIM ** -0.5
    for h in range(N_HEADS):
        half = h % 2
        qbd_ref[h // 2, HEAD_DIM * half:HEAD_DIM * (half + 1), LANES * half:LANES * (half + 1)] = (
            q_t[HEAD_DIM * h:HEAD_DIM * (h + 1)].astype(BF16))

    n_chunks = ((i + 1) * ATT_QB + ATT_KC - 1) // ATT_KC

    def score_chunk(c, masked):
        r0 = pl.multiple_of(c * ATT_KC, ATT_KC)
        ks = ki4_ref[pl.ds(r0, ATT_KC), :]
        acc = jnp.zeros((ATT_KC, LANES), F32)
        for p in range(IDX_HEADS // 2):
            d = _dot(ks, qstat_ref[p])
            acc = acc + w_ref[2 * p:2 * p + 1, :] * jnp.maximum(d[:, 0:LANES], 0.0)
            acc = acc + w_ref[2 * p + 1:2 * p + 2, :] * jnp.maximum(d[:, LANES:], 0.0)
        key = _sort_key(acc)
        if masked:
            kpos = r0 + lax.broadcasted_iota(I32, (ATT_KC, LANES), 0)
            qpos = i * ATT_QB + lax.broadcasted_iota(I32, (ATT_KC, LANES), 1)
            key = jnp.where(kpos <= qpos, key, INT_MIN)
        st_ref[pl.ds(r0, ATT_KC), :] = key

    def full_chunk(c, carry):
        score_chunk(c, False)
        return carry

    lax.fori_loop(0, n_chunks - 1, full_chunk, 0)
    score_chunk(n_chunks - 1, True)

    assert ATT_KC == KTH_ROWS
    thr = jnp.maximum(_kth_largest_key(st_ref, n_chunks, topk), INT_MIN + 1)

    m_ref[...] = jnp.full_like(m_ref, NEG_INF)
    l_ref[...] = jnp.zeros_like(l_ref)
    acc_ref[...] = jnp.zeros_like(acc_ref)

    def attend_chunk(c, carry):
        r0 = pl.multiple_of(c * ATT_KC, ATT_KC)
        bias = jnp.where(st_ref[pl.ds(r0, ATT_KC), :] >= thr, 0.0, NEG_INF)
        m_old = m_ref[...]
        ds = [_dot(k_ref[pl.ds(r0, ATT_KC), LANES * p:LANES * (p + 1)], qbd_ref[p])
              for p in range(N_HEADS // 2)]
        s = [ds[h // 2][:, LANES * (h % 2):LANES * (h % 2 + 1)] + bias for h in range(N_HEADS)]
        m_new = jnp.maximum(m_old, jnp.concatenate([jnp.max(x, axis=0, keepdims=True) for x in s], axis=0))
        m_safe = jnp.where(m_new == NEG_INF, 0.0, m_new)
        alpha = jnp.exp(m_old - m_safe)
        pr = [jnp.exp(s[h] - m_safe[h:h + 1, :]) for h in range(N_HEADS)]
        l_ref[...] = alpha * l_ref[...] + jnp.concatenate(
            [jnp.sum(x, axis=0, keepdims=True) for x in pr], axis=0)
        m_ref[...] = m_new
        for h in range(N_HEADS):
            rows = slice(HEAD_DIM * h, HEAD_DIM * (h + 1))
            pv = _dot(vt_ref[rows, pl.ds(r0, ATT_KC)], pr[h].astype(BF16))
            acc_ref[rows, :] = alpha[h:h + 1, :] * acc_ref[rows, :] + pv
        return carry

    lax.fori_loop(0, n_chunks, attend_chunk, 0)
    out_t = jnp.concatenate(
        [acc_ref[HEAD_DIM * h:HEAD_DIM * (h + 1), :] / l_ref[h:h + 1, :] for h in range(N_HEADS)], axis=0)
    o_ref[...] = out_t.T


def _attn_prompt(qi, wb, q, ki4, kbf, vt, topk):
    s = q.shape[0]
    assert s % ATT_KC == 0
    row = lambda n: pl.BlockSpec((ATT_QB, n), lambda i: (i, 0))
    resident = pl.BlockSpec(memory_space=pltpu.VMEM)
    vm = lambda shape, dt: pltpu.VMEM(shape, dt)
    return pl.pallas_call(
        functools.partial(_attn_prompt_body, topk=topk),
        grid=(s // ATT_QB,),
        in_specs=[row(1024), row(LANES), row(ATT_WIDTH), resident, resident, resident],
        out_specs=row(ATT_WIDTH),
        out_shape=jax.ShapeDtypeStruct((s, ATT_WIDTH), F32),
        scratch_shapes=[vm((s, LANES), I32), vm((IDX_HEADS // 2, 2 * LANES, 2 * LANES), BF16),
                        vm((N_HEADS // 2, LANES, 2 * LANES), BF16), vm((IDX_HEADS, LANES), F32),
                        vm((ATT_WIDTH, LANES), F32), vm((N_HEADS, LANES), F32), vm((N_HEADS, LANES), F32)],
        compiler_params=_params(("arbitrary",), 60 * 1024 * 1024),
        name="attn_prompt",
    )(qi, wb, q, ki4, kbf, vt)


PAGE_GROUP = 8
SC_PAD_ROWS = 8
ATT_PAGES = 16


def _sample_scores_body(pt_ref, qi_ref, w_ref, kin_ref, ik_hbm, sc_ref, buf_ref, sem_ref, *, layer, n_pages):
    b = pl.program_id(0)
    nb = pl.num_programs(0)

    def page_copy(seq, p, slot):
        return pltpu.make_async_copy(ik_hbm.at[layer, pt_ref[seq, p]],
                                     buf_ref.at[slot, :, pl.ds(p * PAGE_SIZE, PAGE_SIZE)], sem_ref.at[slot])

    def issue(seq, slot):
        for p in range(n_pages):
            page_copy(seq, p, slot).start()

    @pl.when(b == 0)
    def _():
        issue(0, 0)

    @pl.when(b + 1 < nb)
    def _():
        issue(b + 1, (b + 1) % 2)

    slot = b % 2
    for p in range(n_pages):
        page_copy(b, p, slot).wait()

    qi = qi_ref[0]
    qh, ql = _split(qi)
    w = w_ref[0] * (IDX_DIM ** -0.5 * IDX_HEADS ** -0.5)
    cols = PAGE_GROUP * PAGE_SIZE
    for g in range(n_pages // PAGE_GROUP):
        kh, kl = _split(buf_ref[slot, :, cols * g:cols * (g + 1)])
        d = _dot(qh, kh) + _dot(ql, kh) + _dot(qh, kl)
        s = jnp.sum(w * jnp.maximum(d, 0.0), axis=0, keepdims=True)
        for p in range(PAGE_GROUP):
            sc_ref[0, PAGE_GROUP * g + p:PAGE_GROUP * g + p + 1, :] = s[:, PAGE_SIZE * p:PAGE_SIZE * (p + 1)]
    dn = jnp.sum(qi * kin_ref[0], axis=1, keepdims=True)
    sn = jnp.sum(w * jnp.maximum(dn, 0.0), axis=0, keepdims=True)
    pad = jnp.where((lax.broadcasted_iota(I32, (SC_PAD_ROWS, LANES), 0) == 0)
                    & (lax.broadcasted_iota(I32, (SC_PAD_ROWS, LANES), 1) == 0), sn, NEG_INF)
    sc_ref[0, n_pages:, :] = pad


def _sample_scores(page_table, qi, wi, ki_new, cache_ik_t, layer):
    nb, n_pages = page_table.shape
    assert n_pages % PAGE_GROUP == 0
    rows = n_pages + SC_PAD_ROWS
    blk = lambda a: pl.BlockSpec((1,) + a.shape[1:], lambda b, pt: (b, 0, 0))
    qi3 = qi.reshape(nb, IDX_HEADS, IDX_DIM)
    w3 = wi.reshape(nb, IDX_HEADS, 1)
    kin3 = ki_new.reshape(nb, 1, IDX_DIM)
    return pl.pallas_call(
        functools.partial(_sample_scores_body, layer=layer, n_pages=n_pages),
        grid_spec=pltpu.PrefetchScalarGridSpec(
            num_scalar_prefetch=1,
            grid=(nb,),
            in_specs=[blk(qi3), blk(w3), blk(kin3), pl.BlockSpec(memory_space=pl.ANY)],
            out_specs=pl.BlockSpec((1, rows, LANES), lambda b, pt: (b, 0, 0)),
            scratch_shapes=[pltpu.VMEM((2, IDX_DIM, n_pages * PAGE_SIZE), F32), pltpu.SemaphoreType.DMA((2,))],
        ),
        out_shape=jax.ShapeDtypeStruct((nb, rows, LANES), F32),
        compiler_params=_params(("arbitrary",)),
        name="sample_scores",
    )(page_table, qi3, w3, kin3, cache_ik_t)


def _sample_select_body(sc_ref, thr_ref, st_ref, *, topk):
    n = sc_ref.shape[0]
    ngroups = n // KTH_ROWS + pl.program_id(0)

    def to_keys(g, carry):
        rows = pl.ds(pl.multiple_of(g * KTH_ROWS, KTH_ROWS), KTH_ROWS)
        x = sc_ref[rows, :]
        st_ref[rows, :] = jnp.where(x == NEG_INF, INT_MIN, _sort_key(x))
        return carry

    lax.fori_loop(0, ngroups, to_keys, 0)
    thr_ref[...] = jnp.maximum(_kth_largest_key(st_ref, ngroups, topk), INT_MIN + 1)


def _sample_select(sc_t, topk):
    n, nb = sc_t.shape
    assert nb == LANES and n % KTH_ROWS == 0
    whole = lambda shape: pl.BlockSpec(shape, lambda i: (0, 0))
    return pl.pallas_call(
        functools.partial(_sample_select_body, topk=topk),
        grid=(1,),
        in_specs=[whole((n, nb))],
        out_specs=whole((1, nb)),
        out_shape=jax.ShapeDtypeStruct((1, nb), I32),
        scratch_shapes=[pltpu.VMEM((n, nb), I32)],
        compiler_params=_params(("arbitrary",)),
        name="sample_select",
    )(sc_t)


def _sample_attend_body(pt_ref, thr_ref, sc_ref, qt_ref, q_ref, kn_ref, vn_ref, ck_hbm, cv_hbm, o_ref,
                        kbuf_ref, vbuf_ref, sem_ref, qb_ref, acc_ref, m_ref, l_ref, *, layer, n_pages):
    b = pl.program_id(0)
    c = pl.program_id(1)
    n_chunks = pl.num_programs(1)
    step = b * n_chunks + c
    n_steps = pl.num_programs(0) * n_chunks

    def copies(st, slot):
        seq, chunk = st // n_chunks, st % n_chunks
        out = []
        for p in range(ATT_PAGES):
            page = pt_ref[seq, chunk * ATT_PAGES + p]
            out.append(pltpu.make_async_copy(ck_hbm.at[layer, page], kbuf_ref.at[slot, p], sem_ref.at[slot]))
            out.append(pltpu.make_async_copy(cv_hbm.at[layer, page], vbuf_ref.at[slot, p], sem_ref.at[slot]))
        return out

    @pl.when(step == 0)
    def _():
        for cp in copies(0, 0):
            cp.start()

    @pl.when(step + 1 < n_steps)
    def _():
        for cp in copies(step + 1, (step + 1) % 2):
            cp.start()

    slot = step % 2
    for cp in copies(step, slot):
        cp.wait()

    @pl.when(c == 0)
    def _():
        m_ref[...] = jnp.full_like(m_ref, NEG_INF)
        l_ref[...] = jnp.zeros_like(l_ref)
        acc_ref[...] = jnp.zeros_like(acc_ref)
        qt = qt_ref[0] * HEAD_DIM ** -0.5
        for h in range(N_HEADS):
            qb_ref[h] = jnp.broadcast_to(qt[:, h:h + 1], (HEAD_DIM, LANES))

    thr = thr_ref[b]
    rows = pl.ds(pl.multiple_of(c * ATT_PAGES, ATT_PAGES), ATT_PAGES)
    sel = _sort_key(sc_ref[0, rows, :]) >= thr
    for h in range(N_HEADS):
        qb = qb_ref[h]
        s = jnp.concatenate([jnp.sum(kbuf_ref[slot, p, h] * qb, axis=0, keepdims=True)
                             for p in range(ATT_PAGES)], axis=0)
        s = jnp.where(sel, s, NEG_INF)
        m_old = m_ref[h:h + 1, :]
        m_new = jnp.maximum(m_old, jnp.max(jnp.max(s, axis=0, keepdims=True), axis=1, keepdims=True))
        m_safe = jnp.where(m_new == NEG_INF, 0.0, m_new)
        alpha = jnp.exp(m_old - m_safe)
        pr = jnp.exp(s - m_safe)
        l_ref[h:h + 1, :] = alpha * l_ref[h:h + 1, :] + jnp.sum(
            jnp.sum(pr, axis=0, keepdims=True), axis=1, keepdims=True)
        acc = alpha * acc_ref[h]
        for p in range(ATT_PAGES):
            acc = acc + pr[p:p + 1, :] * vbuf_ref[slot, p, h]
        acc_ref[h] = acc
        m_ref[h:h + 1, :] = m_new

    @pl.when(c == n_chunks - 1)
    def _():
        vn = vn_ref[0]
        s_new = jnp.sum(q_ref[0] * kn_ref[0], axis=1, keepdims=True) * HEAD_DIM ** -0.5
        new_sel = _sort_key(sc_ref[0, n_pages:n_pages + 1, 0:1]) >= thr
        s_new = jnp.where(new_sel, s_new, NEG_INF)
        m_old = m_ref[:, 0:1]
        m_fin = jnp.maximum(m_old, s_new)
        m_safe = jnp.where(m_fin == NEG_INF, 0.0, m_fin)
        alpha = jnp.exp(m_old - m_safe)
        p_new = jnp.exp(s_new - m_safe)
        num = jnp.sum(acc_ref[...], axis=2) * alpha + p_new * vn
        o_ref[0] = num / (l_ref[:, 0:1] * alpha + p_new)


def _sample_attend(page_table, thr, sc, q, k_new, v_new, cache_k_t, cache_v_t, layer):
    nb, n_pages = page_table.shape
    assert n_pages % ATT_PAGES == 0
    q3, kn3, vn3 = (a.reshape(nb, N_HEADS, HEAD_DIM) for a in (q, k_new, v_new))
    qt3 = q3.transpose(0, 2, 1)
    blk = pl.BlockSpec((1, N_HEADS, HEAD_DIM), lambda b, c, *_: (b, 0, 0))
    hbm = pl.BlockSpec(memory_space=pl.ANY)
    page_buf = pltpu.VMEM((2, ATT_PAGES, N_HEADS, HEAD_DIM, PAGE_SIZE), F32)
    out = pl.pallas_call(
        functools.partial(_sample_attend_body, layer=layer, n_pages=n_pages),
        grid_spec=pltpu.PrefetchScalarGridSpec(
            num_scalar_prefetch=2,
            grid=(nb, n_pages // ATT_PAGES),
            in_specs=[pl.BlockSpec((1,) + sc.shape[1:], lambda b, c, *_: (b, 0, 0)),
                      pl.BlockSpec((1, HEAD_DIM, N_HEADS), lambda b, c, *_: (b, 0, 0)),
                      blk, blk, blk, hbm, hbm],
            out_specs=blk,
            scratch_shapes=[page_buf, page_buf, pltpu.SemaphoreType.DMA((2,)),
                            pltpu.VMEM((N_HEADS, HEAD_DIM, LANES), F32),
                            pltpu.VMEM((N_HEADS, HEAD_DIM, LANES), F32),
                            pltpu.VMEM((N_HEADS, LANES), F32), pltpu.VMEM((N_HEADS, LANES), F32)],
        ),
        out_shape=jax.ShapeDtypeStruct((nb, N_HEADS, HEAD_DIM), F32),
        compiler_params=_params(("arbitrary", "arbitrary")),
        name="sample_attend",
    )(page_table, thr, sc, qt3, q3, kn3, vn3, cache_k_t, cache_v_t)
    return out.reshape(nb, ATT_WIDTH)


def _attn_sample(z, page_table, cache_k, cache_v, cache_ik, layer, topk):
    nb = page_table.shape[0]
    ki_new = z["ka"][:, :IDX_DIM]
    wi = z["wb"][:, :IDX_HEADS]
    sc = _sample_scores(page_table, z["qi"], wi, ki_new, cache_ik.transpose(0, 1, 3, 2), layer)
    thr = _sample_select(sc.reshape(nb, -1).T, topk)
    return _sample_attend(page_table, thr.reshape(nb), sc, z["q"], z["k"], z["v"],
                          cache_k.transpose(0, 1, 3, 4, 2), cache_v.transpose(0, 1, 3, 4, 2), layer)


S5_SUB = 64
S5_CHUNK = SUBLANES * S5_SUB
S5_STRIP = 512


def _s5_prep_body(are_ref, aim_ref, ldt_ref, bre_ref, bim_ref, cre_ref, cim_ref,
                  bbh_ref, bbl_ref, cc_ref, abar_ref, abarl_ref):
    ar = are_ref[...]
    ai = aim_ref[...]
    dt = jnp.exp(ldt_ref[...])
    mag = jnp.exp(dt * ar)
    abr = mag * jnp.cos(dt * ai)
    abi = mag * jnp.sin(dt * ai)
    den = ar * ar + ai * ai
    xr = abr - 1.0
    fre = (xr * ar + abi * ai) / den
    fim = (abi * ar - xr * ai) / den
    bre = bre_ref[...]
    bim = bim_ref[...]
    bb_re = fre * bre - fim * bim
    bb_im = fre * bim + fim * bre
    h, lo = _split(bb_re)
    bbh_ref[:, 0:SSM_FLAT] = h
    bbl_ref[:, 0:SSM_FLAT] = lo
    h, lo = _split(bb_im)
    bbh_ref[:, SSM_FLAT:] = h
    bbl_ref[:, SSM_FLAT:] = lo
    cc_ref[0:SSM_FLAT, :] = cre_ref[...].astype(BF16)
    cc_ref[SSM_FLAT:, :] = (-cim_ref[...]).astype(BF16)
    abar_ref[0:1, :] = abr
    abar_ref[1:2, :] = abi
    pr, pi = abr, abi
    for _ in range(int(math.log2(S5_SUB))):
        pr, pi = pr * pr - pi * pi, 2.0 * pr * pi
    abarl_ref[0:1, :] = pr
    abarl_ref[1:2, :] = pi


def _s5_prep(a_re, a_im, log_dt, b_re, b_im, c_re, c_im):
    eye = jnp.eye(SSM_GROUPS, dtype=F32)
    bd_b = lambda b: jnp.einsum("gpc,gh->gchp", b, eye).reshape(SSM_WIDTH, SSM_FLAT)
    bd_c = lambda c: jnp.einsum("gcp,gh->gphc", c, eye).reshape(SSM_FLAT, SSM_WIDTH)
    row = lambda a: a.reshape(1, SSM_FLAT)
    ldt = jnp.repeat(log_dt, SSM_STATE).reshape(1, SSM_FLAT)
    args = (row(a_re), row(a_im), ldt, bd_b(b_re), bd_b(b_im), bd_c(c_re), bd_c(c_im))
    return pl.pallas_call(
        _s5_prep_body,
        out_shape=[jax.ShapeDtypeStruct((SSM_WIDTH, 2 * SSM_FLAT), BF16),
                   jax.ShapeDtypeStruct((SSM_WIDTH, 2 * SSM_FLAT), BF16),
                   jax.ShapeDtypeStruct((2 * SSM_FLAT, SSM_WIDTH), BF16),
                   jax.ShapeDtypeStruct((2, SSM_FLAT), F32),
                   jax.ShapeDtypeStruct((2, SSM_FLAT), F32)],
        compiler_params=_params(None),
        name="s5_prep",
    )(*args)


def _s5_scan_body(u_ref, bbh_ref, cc_ref, abar_ref, abarl_ref, d_ref, y_ref, st_ref,
                  buf_ref, ends_ref, carry_ref):
    @pl.when(pl.program_id(0) == 0)
    def _():
        carry_ref[...] = jnp.zeros_like(carry_ref)

    u = u_ref[...]
    buf_ref[...] = _dot(u.astype(BF16), bbh_ref[...])

    def sweep(store):
        for s in range(SSM_FLAT // S5_STRIP):
            re = slice(s * S5_STRIP, (s + 1) * S5_STRIP)
            im = slice(SSM_FLAT + s * S5_STRIP, SSM_FLAT + (s + 1) * S5_STRIP)
            ar = jnp.broadcast_to(abar_ref[0:1, re], (SUBLANES, S5_STRIP))
            ai = jnp.broadcast_to(abar_ref[1:2, re], (SUBLANES, S5_STRIP))

            def step(i, c):
                sr, si = c
                rows = pl.ds(pl.multiple_of(i * SUBLANES, SUBLANES), SUBLANES)
                nr = ar * sr - ai * si + buf_ref[rows, re]
                ni = ar * si + ai * sr + buf_ref[rows, im]
                if store:
                    buf_ref[rows, re] = nr
                    buf_ref[rows, im] = ni
                return nr, ni

            if store:
                init = (ends_ref[:, re], ends_ref[:, im])
            else:
                init = (jnp.zeros((SUBLANES, S5_STRIP), F32),) * 2
            sr, si = lax.fori_loop(0, S5_SUB, step, init, unroll=4)
            if not store:
                ends_ref[:, re] = sr
                ends_ref[:, im] = si

    sweep(False)
    pr = abarl_ref[0:1, :]
    pi = abarl_ref[1:2, :]
    cr = carry_ref[0:1, 0:SSM_FLAT]
    ci = carry_ref[0:1, SSM_FLAT:]
    for j in range(SUBLANES):
        er = ends_ref[j:j + 1, 0:SSM_FLAT]
        ei = ends_ref[j:j + 1, SSM_FLAT:]
        ends_ref[j:j + 1, 0:SSM_FLAT] = cr
        ends_ref[j:j + 1, SSM_FLAT:] = ci
        cr, ci = pr * cr - pi * ci + er, pr * ci + pi * cr + ei
    carry_ref[0:1, 0:SSM_FLAT] = cr
    carry_ref[0:1, SSM_FLAT:] = ci
    sweep(True)
    y_ref[...] = _dot(buf_ref[...].astype(BF16), cc_ref[...]) + d_ref[...] * u
    st_ref[...] = carry_ref[0:1, :]


def _s5_prompt(us_perm, bbh, cc, abar, abarl, d_row):
    t = us_perm.shape[0]
    full = lambda a: pl.BlockSpec(a.shape, lambda i: (0,) * a.ndim)
    return pl.pallas_call(
        _s5_scan_body,
        grid=(t // S5_CHUNK,),
        in_specs=[pl.BlockSpec((S5_CHUNK, SSM_WIDTH), lambda i: (i, 0)),
                  full(bbh), full(cc), full(abar), full(abarl), full(d_row)],
        out_specs=[pl.BlockSpec((S5_CHUNK, SSM_WIDTH), lambda i: (i, 0)),
                   pl.BlockSpec((1, 2 * SSM_FLAT), lambda i: (0, 0))],
        out_shape=[jax.ShapeDtypeStruct((t, SSM_WIDTH), F32),
                   jax.ShapeDtypeStruct((1, 2 * SSM_FLAT), F32)],
        scratch_shapes=[pltpu.VMEM((S5_CHUNK, 2 * SSM_FLAT), F32),
                        pltpu.VMEM((SUBLANES, 2 * SSM_FLAT), F32),
                        pltpu.VMEM((SUBLANES, 2 * SSM_FLAT), F32)],
        compiler_params=_params(("arbitrary",)),
        name="s5_scan",
    )(us_perm, bbh, cc, abar, abarl, d_row)


def _s5_step_body(u_ref, hre_ref, him_ref, bbh_ref, bbl_ref, cc_ref, abar_ref, d_ref,
                  y_ref, ore_ref, oim_ref):
    u = u_ref[...]
    uh, ul = _split(u)
    bu = _dot(uh, bbh_ref[...]) + _dot(ul, bbh_ref[...]) + _dot(uh, bbl_ref[...])
    ar = abar_ref[0:1, :]
    ai = abar_ref[1:2, :]
    hr = hre_ref[...]
    hi = him_ref[...]
    nr = ar * hr - ai * hi + bu[:, 0:SSM_FLAT]
    ni = ar * hi + ai * hr + bu[:, SSM_FLAT:]
    ore_ref[...] = nr
    oim_ref[...] = ni
    hcat = jnp.concatenate([nr, ni], axis=1).astype(BF16)
    y_ref[...] = _dot(hcat, cc_ref[...]) + d_ref[...] * u


def _s5_sample(us, h_re, h_im, bbh, bbl, cc, abar, d_row):
    n = us.shape[0]
    return pl.pallas_call(
        _s5_step_body,
        out_shape=[jax.ShapeDtypeStruct((n, SSM_WIDTH), F32),
                   jax.ShapeDtypeStruct((n, SSM_FLAT), F32),
                   jax.ShapeDtypeStruct((n, SSM_FLAT), F32)],
        compiler_params=_params(None),
        name="s5_step",
    )(us, h_re, h_im, bbh, bbl, cc, abar, d_row)


def _s5_permute(x, inverse=False):
    t, w = x.shape
    a, b = (S5_SUB, SUBLANES) if inverse else (SUBLANES, S5_SUB)
    return x.reshape(t // S5_CHUNK, a, b, w).transpose(0, 2, 1, 3).reshape(t, w)


def _mix_body(att_ref, y_ref, ga_ref, gb_ref, h_ref, watt_ref, wglu_ref, wout_ref, gffn_ref,
              h1_ref, unh_ref, unl_ref):
    branch_a = _dot(att_ref[...].astype(BF16), watt_ref[...])
    zs = _gelu(y_ref[...]).astype(BF16)
    glu = _dot(zs, wglu_ref[...])
    branch_b = glu[:, 0:D_MODEL] * _sigmoid(glu[:, D_MODEL:])
    mixed = _sigmoid(ga_ref[...]) * branch_a + _sigmoid(gb_ref[...]) * branch_b
    h1 = h_ref[...] + _dot(mixed.astype(BF16), wout_ref[...])
    h1_ref[...] = h1
    un_t = _rmsnorm(h1, gffn_ref[...]).T
    hi, lo = _split(un_t)
    unh_ref[...] = hi
    unl_ref[...] = lo


def _mix(att, y, ga, gb, h, watt, wglu, wout, gffn, tb):
    t = h.shape[0]
    row = lambda n: pl.BlockSpec((tb, n), lambda i: (i, 0))
    full = lambda a: pl.BlockSpec(a.shape, lambda i: (0,) * a.ndim)
    col = pl.BlockSpec((D_MODEL, tb), lambda i: (0, i))
    return pl.pallas_call(
        _mix_body,
        grid=(t // tb,),
        in_specs=[row(512), row(512), row(1024), row(1024), row(1024),
                  full(watt), full(wglu), full(wout), full(gffn)],
        out_specs=[row(1024), col, col],
        out_shape=[jax.ShapeDtypeStruct((t, D_MODEL), F32),
                   jax.ShapeDtypeStruct((D_MODEL, t), BF16),
                   jax.ShapeDtypeStruct((D_MODEL, t), BF16)],
        compiler_params=_params(("arbitrary",)),
        name="mix",
    )(att, y, ga, gb, h, watt, wglu, wout, gffn)


PEER_ET = 1024
PEER_I0 = PEER_ET // PEER_NKEYS
PEER_HJ = PEER_HEADS * 2


def _top_extract(xs, iota, rounds, emit):
    xs = list(xs)
    n = xs[0].shape[0]
    for r in range(rounds):
        for k, x in enumerate(xs):
            m = jnp.max(x, axis=0, keepdims=True)
            emit(k, r, m)
            if r + 1 < rounds:
                first = jnp.min(jnp.where(x == m, iota, n), axis=0, keepdims=True)
                xs[k] = jnp.where(iota == first, NEG_INF, x)


_PAIR_ROWS = ((0, 16), (1, 8), (2, 5), (3, 4), (4, 3), (5, 2), (6, 2), (7, 2))


def _pair_candidates(sv0, sv1):
    blocks = []
    for a, lim in _PAIR_ROWS:
        n = 16 if lim > SUBLANES else SUBLANES
        blk = sv0[a:a + 1] + sv1[0:n]
        if lim < n:
            blk = jnp.where(lax.broadcasted_iota(I32, blk.shape, 0) < lim, blk, NEG_INF)
        blocks.append(blk)
    blocks.append(sv0[SUBLANES:2 * SUBLANES] + sv1[0:1])
    return jnp.concatenate(blocks, axis=0)


def _peer_body(unh_ref, unl_ref, h1_ref, wqh_ref, wql_ref, keys4_ref, u_ref, vt_ref, o_ref,
               s_ref, e_ref, sv_ref, thr_ref, rz_ref, a_ref, g_ref, acc_ref, *, tb):
    j = pl.program_id(1)
    nlt = tb // LANES

    def lane_tile(lt):
        return pl.ds(pl.multiple_of(lt * LANES, LANES), LANES)

    @pl.when(j == 0)
    def _():
        unh = unh_ref[...]
        wqh = wqh_ref[...]
        qt = _dot(wqh, unh) + _dot(wqh, unl_ref[...]) + _dot(wql_ref[...], unh)
        for hj in range(PEER_HJ):
            xh, xl = _split(qt[PEER_HALF * hj:PEER_HALF * (hj + 1)])
            s_ref[hj] = _dot(keys4_ref[hj % 2], jnp.concatenate([xh, xl, xh, xl], axis=0))

        iota1 = lax.broadcasted_iota(I32, (PEER_NKEYS, LANES), 0)

        def stage1(idx, carry):
            lanes = lane_tile(idx % nlt)
            hjs = (2 * (idx // nlt), 2 * (idx // nlt) + 1)
            xs = [s_ref[hj, :, lanes] for hj in hjs]

            def emit(k, r, m):
                sv_ref[hjs[k], r:r + 1, lanes] = m

            _top_extract(xs, iota1, PEER_TOPK, emit)
            for hj, x in zip(hjs, xs):
                e_ref[hj, :, lanes] = jnp.exp(x - sv_ref[hj, 0:1, lanes])
            return carry

        lax.fori_loop(0, PEER_HEADS * nlt, stage1, 0)

        n_cand = SUBLANES * (len(_PAIR_ROWS) + 2)
        iota2 = lax.broadcasted_iota(I32, (n_cand, LANES), 0)

        def stage2(idx, carry):
            lanes = lane_tile(idx % nlt)
            hs = (2 * (idx // nlt), 2 * (idx // nlt) + 1)
            cands = [_pair_candidates(sv_ref[2 * h, :, lanes], sv_ref[2 * h + 1, :, lanes]) for h in hs]
            tops = ([], [])
            _top_extract(cands, iota2, PEER_TOPK, lambda k, r, m: tops[k].append(m))
            for h, top in zip(hs, tops):
                z = sum(jnp.exp(c - top[0]) for c in top)
                thr_ref[h, :, lanes] = top[-1]
                rz_ref[h, :, lanes] = 1.0 / z
            return carry

        lax.fori_loop(0, PEER_HEADS // 2 * nlt, stage2, 0)
        acc_ref[...] = jnp.zeros_like(acc_ref)

    a_ref[...] = _dot(u_ref[...], unh_ref[...])

    def dense(lt, carry):
        lanes = lane_tile(lt)
        grp = pl.ds(pl.multiple_of(j * PEER_I0, SUBLANES), PEER_I0)
        s0g = [s_ref[2 * h, grp, lanes] for h in range(PEER_HEADS)]
        e0g = [e_ref[2 * h, grp, lanes] * rz_ref[h, :, lanes] for h in range(PEER_HEADS)]
        for r in range(PEER_I0):
            w = jnp.zeros((PEER_NKEYS, LANES), F32)
            for h in range(PEER_HEADS):
                sel = (s_ref[2 * h + 1, :, lanes] + s0g[h][r:r + 1]) >= thr_ref[h, :, lanes]
                w = w + jnp.where(sel, e_ref[2 * h + 1, :, lanes] * e0g[h][r:r + 1], 0.0)
            rows = slice(r * PEER_NKEYS, (r + 1) * PEER_NKEYS)
            g_ref[rows, lanes] = (_gelu(a_ref[rows, lanes]) * w).astype(BF16)
        return carry

    lax.fori_loop(0, nlt, dense, 0)
    acc_ref[...] += _dot(vt_ref[...], g_ref[...])

    @pl.when(j == pl.num_programs(1) - 1)
    def _():
        o_ref[...] = h1_ref[...] + acc_ref[...].T


def _peer(unh, unl, h1, wqh, wql, keys4, u_bf, vt_bf, tb):
    t = h1.shape[0]
    col = pl.BlockSpec((D_MODEL, tb), lambda i, j: (0, i))
    full = lambda a: pl.BlockSpec(a.shape, lambda i, j: (0,) * a.ndim)
    vm = lambda shape, dt: pltpu.VMEM(shape, dt)
    return pl.pallas_call(
        functools.partial(_peer_body, tb=tb),
        grid=(t // tb, PEER_EXPERTS // PEER_ET),
        in_specs=[col, col, pl.BlockSpec((tb, D_MODEL), lambda i, j: (i, 0)),
                  full(wqh), full(wql), full(keys4),
                  pl.BlockSpec((PEER_ET, D_MODEL), lambda i, j: (j, 0)),
                  pl.BlockSpec((D_MODEL, PEER_ET), lambda i, j: (0, j))],
        out_specs=pl.BlockSpec((tb, D_MODEL), lambda i, j: (i, 0)),
        out_shape=jax.ShapeDtypeStruct((t, D_MODEL), F32),
        scratch_shapes=[vm((PEER_HJ, PEER_NKEYS, tb), F32), vm((PEER_HJ, PEER_NKEYS, tb), F32),
                        vm((PEER_HJ, PEER_TOPK, tb), F32), vm((PEER_HEADS, 1, tb), F32),
                        vm((PEER_HEADS, 1, tb), F32), vm((PEER_ET, tb), F32), vm((PEER_ET, tb), BF16),
                        vm((D_MODEL, tb), F32)],
        compiler_params=_params(("arbitrary", "arbitrary")),
        name="peer",
    )(unh, unl, h1, wqh, wql, keys4, u_bf, vt_bf)


def _prep_peer_weights(peer_wq, peer_keys, peer_u, peer_v):
    wqt = peer_wq.T
    wqh = wqt.astype(BF16)
    wql = (wqt - wqh.astype(F32)).astype(BF16)
    kh = peer_keys.astype(BF16)
    kl = (peer_keys - kh.astype(F32)).astype(BF16)
    keys4 = jnp.concatenate([kh, kh, kl, kl], axis=-1)
    return wqh, wql, keys4, peer_u.astype(BF16), peer_v.T.astype(BF16)


def _ple_body(h_ref, p_ref, gple_ref, wgate_ref, wproj_ref, gfin_ref, y_ref):
    h = h_ref[...]
    gate = _sigmoid(_dot(_rmsnorm(h, gple_ref[...]).astype(BF16), wgate_ref[...]))
    h = h + gate * _dot(p_ref[...].astype(BF16), wproj_ref[...])
    y_ref[...] = _rmsnorm(h, gfin_ref[...])


def _ple(h, p, gple, wgate, wproj, gfin, tb):
    t = h.shape[0]
    row = lambda n: pl.BlockSpec((tb, n), lambda i: (i, 0))
    full = lambda a: pl.BlockSpec(a.shape, lambda i: (0,) * a.ndim)
    return pl.pallas_call(
        _ple_body,
        grid=(t // tb,),
        in_specs=[row(D_MODEL), row(p.shape[1]), full(gple), full(wgate), full(wproj), full(gfin)],
        out_specs=row(D_MODEL),
        out_shape=jax.ShapeDtypeStruct((t, D_MODEL), F32),
        compiler_params=_params(("arbitrary",)),
        name="ple_final",
    )(h, p, gple, wgate, wproj, gfin)


def kernel(x_prompt, x_sample, p_prompt, p_sample, cache_k, cache_v, cache_idx_k, state_ssm_re, state_ssm_im, page_table, norm_mix, w_in, ssm_a_re, ssm_a_im, ssm_log_dt, ssm_b_re, ssm_b_im, ssm_c_re, ssm_c_im, ssm_d, w_glu, w_att_br, w_out, norm_ffn, peer_wq, peer_keys, peer_u, peer_v, norm_ple, ple_gate, ple_proj, norm_final):
    l = 0
    whi, wlo, wfast = _prep_in_weights(w_in[l])
    g = norm_mix[l][None, :]
    zp = _in_projection(x_prompt[0], g, whi, wlo, wfast, 256)
    zs = _in_projection(x_sample[:, 0], g, whi, wlo, wfast, 128)
    s5p, s5s = _s5_all(zp["us"], zs["us"], state_ssm_re[l], state_ssm_im[l], ssm_a_re[l], ssm_a_im[l],
                       ssm_log_dt[l], ssm_b_re[l], ssm_b_im[l], ssm_c_re[l], ssm_c_im[l], ssm_d[l])
    seq = x_prompt.shape[1]
    att_p = _attn_prompt(zp["qi"], zp["wb"], zp["q"], zp["ki4"], zp["kbf"], zp["vt"], min(TOPK_MAX, seq // 4))
    watt, wglu, wout = (w.astype(BF16) for w in (w_att_br[l], w_glu[l], w_out[l]))
    pw = _prep_peer_weights(peer_wq[l], peer_keys[l], peer_u[l], peer_v[l])
    wgate, wproj = ple_gate[l].astype(BF16), ple_proj[l].astype(BF16)

    def tail(att, y_ssm, z, h, p, tb):
        h1, unh, unl = _mix(att, y_ssm, z["ga"], z["gb"], h, watt, wglu, wout, norm_ffn[l][None, :], tb)
        h2 = _peer(unh, unl, h1, *pw, tb)
        return _ple(h2, p, norm_ple[l][None, :], wgate, wproj, norm_final[None, :], tb)

    y_p = tail(att_p, s5p[0], zp, x_prompt[0], p_prompt[l, 0], 512)
    nb = x_sample.shape[0]
    past = page_table.shape[1] * PAGE_SIZE
    att_s = _attn_sample(zs, page_table, cache_k, cache_v, cache_idx_k, l, min(TOPK_MAX, (past + 1) // 4))
    y_s = tail(att_s, s5s[0], zs, x_sample[:, 0], p_sample[l, :, 0], 128)

    heads = lambda a, lead: a.reshape(lead + (N_HEADS, HEAD_DIM))
    st = lambda a, lead: a.reshape(lead + (SSM_GROUPS, SSM_STATE))
    return (y_p[None], y_s[:, None],
            heads(zp["k"], (1, 1, seq)), heads(zp["v"], (1, 1, seq)), zp["ka"][:, :IDX_DIM].reshape(1, 1, seq, IDX_DIM),
            heads(zs["k"], (1, nb, 1)), heads(zs["v"], (1, nb, 1)), zs["ka"][:, :IDX_DIM].reshape(1, nb, 1, IDX_DIM),
            st(s5p[1], (1, 1)), st(s5p[2], (1, 1)), st(s5s[1], (1, nb)), st(s5s[2], (1, nb)))


def _s5_all(us_p, us_s, h_re, h_im, a_re, a_im, log_dt, b_re, b_im, c_re, c_im, d):
    bbh, bbl, cc, abar, abarl = _s5_prep(a_re, a_im, log_dt, b_re, b_im, c_re, c_im)
    d_row = d.reshape(1, SSM_WIDTH)
    y_perm, st = _s5_prompt(_s5_permute(us_p), bbh, cc, abar, abarl, d_row)
    y_p = _s5_permute(y_perm, inverse=True)
    n = us_s.shape[0]
    y_s, sre, sim = _s5_sample(us_s, h_re.reshape(n, SSM_FLAT), h_im.reshape(n, SSM_FLAT),
                               bbh, bbl, cc, abar, d_row)
    return (y_p, st[:, :SSM_FLAT], st[:, SSM_FLAT:]), (y_s, sre, sim)
```

```python
import functools
import math

import jax
import jax.numpy as jnp
import numpy as np
from jax import lax
from jax.experimental import pallas as pl
from jax.experimental.pallas import tpu as pltpu

F32 = jnp.float32
BF16 = jnp.bfloat16
I32 = jnp.int32

D_MODEL = 1024
PAGE_SIZE = 128
N_HEADS = 8
HEAD_DIM = 64
ATT_WIDTH = N_HEADS * HEAD_DIM
IDX_HEADS = 16
IDX_DIM = 64
TOPK_MAX = 256
SSM_GROUP = 16
SSM_WIDTH = 512
SSM_GROUPS = SSM_WIDTH // SSM_GROUP
SSM_STATE = 64
SSM_FLAT = SSM_GROUPS * SSM_STATE
PEER_HEADS = 8
PEER_NKEYS = 128
PEER_EXPERTS = PEER_NKEYS * PEER_NKEYS
PEER_HALF = 64
PEER_TOPK = 16
EPS = 1e-6

LANES = 128
SUBLANES = 8
VMEM_LIMIT = 56 * 1024 * 1024
INT_MIN = -(2 ** 31)
NEG_INF = float("-inf")


def _params(sem, vmem=VMEM_LIMIT):
    return pltpu.CompilerParams(dimension_semantics=sem, vmem_limit_bytes=vmem)


def _split(x):
    hi = x.astype(BF16)
    lo = (x - hi.astype(F32)).astype(BF16)
    return hi, lo


def _dot(a, b):
    return jnp.dot(a, b, preferred_element_type=F32)


def _dot_nt(a, b):
    return lax.dot_general(a, b, (((1,), (1,)), ((), ())), preferred_element_type=F32)


def _rmsnorm(x, g):
    return x * lax.rsqrt(jnp.mean(x * x, axis=-1, keepdims=True) + EPS) * g


def _sigmoid(x):
    return 1.0 / (1.0 + jnp.exp(-x))


def _gelu(x):
    return 0.5 * x * (1.0 + lax.erf(x * (2.0 ** -0.5)))


def _sort_key(x):
    b = lax.bitcast_convert_type(x, I32)
    return b ^ ((b >> 31) & 0x7FFFFFFF)


_PREC_COLS = (IDX_HEADS * IDX_DIM, LANES, LANES, SSM_WIDTH)
_FAST_COLS = (ATT_WIDTH, ATT_WIDTH, ATT_WIDTH, D_MODEL, D_MODEL)


def _inproj_prec_body(x_ref, g_ref, whi_ref, wlo_ref, qi_ref, ka_ref, wb_ref, us_ref, ki4_ref):
    u = _rmsnorm(x_ref[...], g_ref[...])
    uh, ul = _split(u)

    def proj(off, w):
        wh = whi_ref[:, off:off + w]
        return _dot(uh, wh) + _dot(ul, wh) + _dot(uh, wlo_ref[:, off:off + w])

    qi_ref[...] = proj(0, 1024)
    ka = proj(1024, LANES)
    ka_ref[...] = ka
    kh, kl = _split(ka)
    ki4_ref[:, 0:LANES] = kh
    ki4_ref[:, LANES:2 * LANES] = kl
    wb_ref[...] = proj(1024 + LANES, LANES)
    us_ref[...] = proj(1024 + 2 * LANES, SSM_WIDTH)


def _inproj_fast_body(x_ref, g_ref, w_ref, q_ref, k_ref, v_ref, ga_ref, gb_ref, kbf_ref, vt_ref):
    u = _rmsnorm(x_ref[...], g_ref[...]).astype(BF16)
    q_ref[...] = _dot(u, w_ref[:, 0:512])
    k = _dot(u, w_ref[:, 512:1024])
    k_ref[...] = k
    kbf_ref[...] = k.astype(BF16)
    v = _dot(u, w_ref[:, 1024:1536])
    v_ref[...] = v
    vt_ref[...] = v.T.astype(BF16)
    ga_ref[...] = _dot(u, w_ref[:, 1536:2560])
    gb_ref[...] = _dot(u, w_ref[:, 2560:3584])


def _in_projection(x, g, w_prec_hi, w_prec_lo, w_fast, tb):
    t = x.shape[0]
    grid = (t // tb,)
    row = lambda n: pl.BlockSpec((tb, n), lambda i: (i, 0))
    full = lambda a: pl.BlockSpec(a.shape, lambda i: (0,) * a.ndim)
    qi, ka, wb, us, ki4 = pl.pallas_call(
        _inproj_prec_body,
        grid=grid,
        in_specs=[row(D_MODEL), full(g), full(w_prec_hi), full(w_prec_lo)],
        out_specs=[row(1024), row(LANES), row(LANES), row(SSM_WIDTH), row(2 * LANES)],
        out_shape=[jax.ShapeDtypeStruct((t, 1024), F32), jax.ShapeDtypeStruct((t, LANES), F32),
                   jax.ShapeDtypeStruct((t, LANES), F32), jax.ShapeDtypeStruct((t, SSM_WIDTH), F32),
                   jax.ShapeDtypeStruct((t, 2 * LANES), BF16)],
        compiler_params=_params(("arbitrary",)),
        name="inproj_prec",
    )(x, g, w_prec_hi, w_prec_lo)
    q, k, v, ga, gb, kbf, vt = pl.pallas_call(
        _inproj_fast_body,
        grid=grid,
        in_specs=[row(D_MODEL), full(g), full(w_fast)],
        out_specs=[row(512), row(512), row(512), row(1024), row(1024), row(512),
                   pl.BlockSpec((512, tb), lambda i: (0, i))],
        out_shape=[jax.ShapeDtypeStruct((t, 512), F32)] * 3 + [jax.ShapeDtypeStruct((t, 1024), F32)] * 2
        + [jax.ShapeDtypeStruct((t, 512), BF16), jax.ShapeDtypeStruct((512, t), BF16)],
        compiler_params=_params(("arbitrary",)),
        name="inproj_fast",
    )(x, g, w_fast)
    return dict(qi=qi, ka=ka, wb=wb, us=us, ki4=ki4, q=q, k=k, v=v, ga=ga, gb=gb, kbf=kbf, vt=vt)


def _prep_in_weights(w_in):
    o = np.cumsum([0, 512, 512, 512, 1024, 64, 16, 512, 1024, 1024])
    wq, wk, wv, wqi, wki, wwi, wus, wga, wgb = [w_in[:, o[j]:o[j + 1]] for j in range(9)]
    wwi_pad = jnp.pad(wwi, ((0, 0), (0, LANES - IDX_HEADS)))
    w_prec = jnp.concatenate([wqi, wki, wki, wwi_pad, wus], axis=1)
    hi = w_prec.astype(BF16)
    lo = (w_prec - hi.astype(F32)).astype(BF16)
    w_fast = jnp.concatenate([wq, wk, wv, wga, wgb], axis=1).astype(BF16)
    return hi, lo, w_fast


KTH_ROWS = 512


def _kth_largest_key(st_ref, ngroups, topk):
    def count_ge(cand):
        cb = jnp.broadcast_to(cand, (SUBLANES, LANES))

        def body(g, accs):
            base = pl.multiple_of(g * KTH_ROWS, KTH_ROWS)
            accs = list(accs)
            for r in range(KTH_ROWS // SUBLANES):
                v = st_ref[pl.ds(base + r * SUBLANES, SUBLANES), :]
                accs[r % 4] = accs[r % 4] + (v >= cb).astype(I32)
            return tuple(accs)

        accs = lax.fori_loop(0, ngroups, body, (jnp.zeros((SUBLANES, LANES), I32),) * 4)
        return jnp.sum(accs[0] + accs[1] + accs[2] + accs[3], axis=0, keepdims=True)

    def cond(c):
        b, _, cnt = c
        return jnp.logical_and(b < 32, jnp.max(cnt) > topk)

    def bit_step(c):
        b, res, cnt = c
        bit = jnp.left_shift(jnp.int32(1), 31 - b)
        cand = jnp.where(b == 0, jnp.zeros_like(res), res | bit)
        n = count_ge(cand)
        take = n >= topk
        return b + 1, jnp.where(take, cand, res), jnp.where(take, n, cnt)

    res0 = jnp.full((1, LANES), INT_MIN, I32)
    _, res, _ = lax.while_loop(cond, bit_step, (jnp.int32(0), res0, count_ge(res0 + 1)))
    return res


ATT_QB = 128
ATT_KC = 512


SOFTMAX_SLACK = 1.01
SOFTMAX_MIN_SUM = 1e-20


def _attn_prompt_body(qi_ref, wb_ref, q_ref, ki4_ref, k_ref, vt_ref, o_ref,
                      st_ref, qstat_ref, qbd_ref, w_ref, acc_ref, m_ref, l_ref, nb_ref, kmx_ref, *, topk):
    i = pl.program_id(0)
    n_keys = k_ref.shape[0]

    @pl.when(i == 0)
    def _():
        qbd_ref[...] = jnp.zeros_like(qbd_ref)
        seg = (lax.broadcasted_iota(I32, (ATT_WIDTH, LANES), 0) // HEAD_DIM
               == lax.broadcasted_iota(I32, (ATT_WIDTH, LANES), 1)).astype(BF16)

        def norm_chunk(c, mx):
            kc = k_ref[pl.ds(pl.multiple_of(c * ATT_KC, ATT_KC), ATT_KC), :].astype(F32)
            return jnp.maximum(mx, jnp.max(_dot((kc * kc).astype(BF16), seg), axis=0, keepdims=True))

        mx = lax.fori_loop(0, n_keys // ATT_KC, norm_chunk, jnp.zeros((1, LANES), F32))
        lane = lax.broadcasted_iota(I32, (1, LANES), 1)
        for h in range(N_HEADS):
            kmx_ref[h:h + 1, :] = jnp.broadcast_to(
                jnp.max(jnp.where(lane == h, mx, 0.0), axis=1, keepdims=True), (1, LANES))

    qi_t = qi_ref[...].T
    for h in range(IDX_HEADS):
        xh, xl = _split(qi_t[IDX_DIM * h:IDX_DIM * (h + 1)])
        c = slice((h % 2) * LANES, (h % 2 + 1) * LANES)
        for r, piece in enumerate((xh, xl, xh, xl)):
            qstat_ref[h // 2, IDX_DIM * r:IDX_DIM * (r + 1), c] = piece
    w_ref[...] = wb_ref[...].T[0:IDX_HEADS] * (IDX_DIM ** -0.5 * IDX_HEADS ** -0.5)
    q_t = q_ref[...].T * HEAD_DIM ** -0.5
    for h in range(N_HEADS):
        half = h % 2
        cols = slice(LANES * half, LANES * (half + 1))
        qh = q_t[HEAD_DIM * h:HEAD_DIM * (h + 1)].astype(BF16)
        qbd_ref[h // 2, HEAD_DIM * half:HEAD_DIM * (half + 1), cols] = qh
        qf = qh.astype(F32)
        bound = jnp.sqrt(jnp.sum(qf * qf, axis=0, keepdims=True) * kmx_ref[h:h + 1, :]) * SOFTMAX_SLACK + 1e-6
        nb_ref[h:h + 1, :] = -bound

    n_chunks = ((i + 1) * ATT_QB + ATT_KC - 1) // ATT_KC

    def score_chunk(c, masked):
        r0 = pl.multiple_of(c * ATT_KC, ATT_KC)
        ks = ki4_ref[pl.ds(r0, ATT_KC), :]
        acc = jnp.zeros((ATT_KC, LANES), F32)
        for p in range(IDX_HEADS // 2):
            d = _dot(ks, qstat_ref[p])
            acc = acc + w_ref[2 * p:2 * p + 1, :] * jnp.maximum(d[:, 0:LANES], 0.0)
            acc = acc + w_ref[2 * p + 1:2 * p + 2, :] * jnp.maximum(d[:, LANES:], 0.0)
        key = _sort_key(acc)
        if masked:
            kpos = r0 + lax.broadcasted_iota(I32, (ATT_KC, LANES), 0)
            qpos = i * ATT_QB + lax.broadcasted_iota(I32, (ATT_KC, LANES), 1)
            key = jnp.where(kpos <= qpos, key, INT_MIN)
        st_ref[pl.ds(r0, ATT_KC), :] = key

    def full_chunk(c, carry):
        score_chunk(c, False)
        return carry

    lax.fori_loop(0, n_chunks - 1, full_chunk, 0)
    score_chunk(n_chunks - 1, True)

    assert ATT_KC == KTH_ROWS
    thr = jnp.maximum(_kth_largest_key(st_ref, n_chunks, topk), INT_MIN + 1)

    l_ref[...] = jnp.zeros_like(l_ref)
    acc_ref[...] = jnp.zeros_like(acc_ref)

    def shifted_chunk(c, carry):
        r0 = pl.multiple_of(c * ATT_KC, ATT_KC)
        bias = jnp.where(st_ref[pl.ds(r0, ATT_KC), :] >= thr, 0.0, NEG_INF)
        ds = [_dot(k_ref[pl.ds(r0, ATT_KC), LANES * p:LANES * (p + 1)], qbd_ref[p])
              for p in range(N_HEADS // 2)]
        pr = [jnp.exp(ds[h // 2][:, LANES * (h % 2):LANES * (h % 2 + 1)] + (bias + nb_ref[h:h + 1, :]))
              for h in range(N_HEADS)]
        l_ref[...] += jnp.concatenate([jnp.sum(x, axis=0, keepdims=True) for x in pr], axis=0)
        pvs = [_dot(vt_ref[HEAD_DIM * h:HEAD_DIM * (h + 1), pl.ds(r0, ATT_KC)], pr[h].astype(BF16))
               for h in range(N_HEADS)]
        acc_ref[...] += jnp.concatenate(pvs, axis=0)
        return carry

    lax.fori_loop(0, n_chunks, shifted_chunk, 0)

    def attend_chunk(c, carry):
        r0 = pl.multiple_of(c * ATT_KC, ATT_KC)
        bias = jnp.where(st_ref[pl.ds(r0, ATT_KC), :] >= thr, 0.0, NEG_INF)
        m_old = m_ref[...]
        ds = [_dot(k_ref[pl.ds(r0, ATT_KC), LANES * p:LANES * (p + 1)], qbd_ref[p])
              for p in range(N_HEADS // 2)]
        s = [ds[h // 2][:, LANES * (h % 2):LANES * (h % 2 + 1)] + bias for h in range(N_HEADS)]
        m_new = jnp.maximum(m_old, jnp.concatenate([jnp.max(x, axis=0, keepdims=True) for x in s], axis=0))
        m_safe = jnp.where(m_new == NEG_INF, 0.0, m_new)
        alpha = jnp.exp(m_old - m_safe)
        pr = [jnp.exp(s[h] - m_safe[h:h + 1, :]) for h in range(N_HEADS)]
        l_ref[...] = alpha * l_ref[...] + jnp.concatenate(
            [jnp.sum(x, axis=0, keepdims=True) for x in pr], axis=0)
        m_ref[...] = m_new
        for h in range(N_HEADS):
            rows = slice(HEAD_DIM * h, HEAD_DIM * (h + 1))
            pv = _dot(vt_ref[rows, pl.ds(r0, ATT_KC)], pr[h].astype(BF16))
            acc_ref[rows, :] = alpha[h:h + 1, :] * acc_ref[rows, :] + pv
        return carry

    @pl.when(jnp.logical_not(jnp.min(l_ref[...]) >= SOFTMAX_MIN_SUM))
    def _():
        m_ref[...] = jnp.full_like(m_ref, NEG_INF)
        l_ref[...] = jnp.zeros_like(l_ref)
        acc_ref[...] = jnp.zeros_like(acc_ref)
        lax.fori_loop(0, n_chunks, attend_chunk, 0)

    out_t = jnp.concatenate(
        [acc_ref[HEAD_DIM * h:HEAD_DIM * (h + 1), :] / l_ref[h:h + 1, :] for h in range(N_HEADS)], axis=0)
    o_ref[...] = out_t.T


def _attn_prompt(qi, wb, q, ki4, kbf, vt, topk):
    s = q.shape[0]
    assert s % ATT_KC == 0
    row = lambda n: pl.BlockSpec((ATT_QB, n), lambda i: (i, 0))
    resident = pl.BlockSpec(memory_space=pltpu.VMEM)
    vm = lambda shape, dt: pltpu.VMEM(shape, dt)
    return pl.pallas_call(
        functools.partial(_attn_prompt_body, topk=topk),
        grid=(s // ATT_QB,),
        in_specs=[row(1024), row(LANES), row(ATT_WIDTH), resident, resident, resident],
        out_specs=row(ATT_WIDTH),
        out_shape=jax.ShapeDtypeStruct((s, ATT_WIDTH), F32),
        scratch_shapes=[vm((s, LANES), I32), vm((IDX_HEADS // 2, 2 * LANES, 2 * LANES), BF16),
                        vm((N_HEADS // 2, LANES, 2 * LANES), BF16), vm((IDX_HEADS, LANES), F32),
                        vm((ATT_WIDTH, LANES), F32), vm((N_HEADS, LANES), F32), vm((N_HEADS, LANES), F32),
                        vm((N_HEADS, LANES), F32), vm((N_HEADS, LANES), F32)],
        compiler_params=_params(("arbitrary",), 60 * 1024 * 1024),
        name="attn_prompt",
    )(qi, wb, q, ki4, kbf, vt)


PAGE_GROUP = 8
SC_PAD_ROWS = 8
ATT_PAGES = 16


def _sample_scores_body(pt_ref, qi_ref, w_ref, kin_ref, ik_hbm, sc_ref, buf_ref, sem_ref, *, layer, n_pages):
    b = pl.program_id(0)
    nb = pl.num_programs(0)

    def page_copy(seq, p, slot):
        return pltpu.make_async_copy(ik_hbm.at[layer, pt_ref[seq, p]],
                                     buf_ref.at[slot, :, pl.ds(p * PAGE_SIZE, PAGE_SIZE)], sem_ref.at[slot])

    def issue(seq, slot):
        for p in range(n_pages):
            page_copy(seq, p, slot).start()

    @pl.when(b == 0)
    def _():
        issue(0, 0)

    @pl.when(b + 1 < nb)
    def _():
        issue(b + 1, (b + 1) % 2)

    slot = b % 2
    for p in range(n_pages):
        page_copy(b, p, slot).wait()

    qi = qi_ref[0]
    qh, ql = _split(qi)
    w = w_ref[0] * (IDX_DIM ** -0.5 * IDX_HEADS ** -0.5)
    cols = PAGE_GROUP * PAGE_SIZE
    for g in range(n_pages // PAGE_GROUP):
        kh, kl = _split(buf_ref[slot, :, cols * g:cols * (g + 1)])
        d = _dot(qh, kh) + _dot(ql, kh) + _dot(qh, kl)
        s = jnp.sum(w * jnp.maximum(d, 0.0), axis=0, keepdims=True)
        for p in range(PAGE_GROUP):
            sc_ref[0, PAGE_GROUP * g + p:PAGE_GROUP * g + p + 1, :] = s[:, PAGE_SIZE * p:PAGE_SIZE * (p + 1)]
    dn = jnp.sum(qi * kin_ref[0], axis=1, keepdims=True)
    sn = jnp.sum(w * jnp.maximum(dn, 0.0), axis=0, keepdims=True)
    pad = jnp.where((lax.broadcasted_iota(I32, (SC_PAD_ROWS, LANES), 0) == 0)
                    & (lax.broadcasted_iota(I32, (SC_PAD_ROWS, LANES), 1) == 0), sn, NEG_INF)
    sc_ref[0, n_pages:, :] = pad


def _sample_scores(page_table, qi, wi, ki_new, cache_ik_t, layer):
    nb, n_pages = page_table.shape
    assert n_pages % PAGE_GROUP == 0
    rows = n_pages + SC_PAD_ROWS
    blk = lambda a: pl.BlockSpec((1,) + a.shape[1:], lambda b, pt: (b, 0, 0))
    qi3 = qi.reshape(nb, IDX_HEADS, IDX_DIM)
    w3 = wi.reshape(nb, IDX_HEADS, 1)
    kin3 = ki_new.reshape(nb, 1, IDX_DIM)
    return pl.pallas_call(
        functools.partial(_sample_scores_body, layer=layer, n_pages=n_pages),
        grid_spec=pltpu.PrefetchScalarGridSpec(
            num_scalar_prefetch=1,
            grid=(nb,),
            in_specs=[blk(qi3), blk(w3), blk(kin3), pl.BlockSpec(memory_space=pl.ANY)],
            out_specs=pl.BlockSpec((1, rows, LANES), lambda b, pt: (b, 0, 0)),
            scratch_shapes=[pltpu.VMEM((2, IDX_DIM, n_pages * PAGE_SIZE), F32), pltpu.SemaphoreType.DMA((2,))],
        ),
        out_shape=jax.ShapeDtypeStruct((nb, rows, LANES), F32),
        compiler_params=_params(("arbitrary",)),
        name="sample_scores",
    )(page_table, qi3, w3, kin3, cache_ik_t)


def _sample_select_body(sc_ref, thr_ref, st_ref, *, topk):
    n = sc_ref.shape[0]
    ngroups = n // KTH_ROWS + pl.program_id(0)

    def to_keys(g, carry):
        rows = pl.ds(pl.multiple_of(g * KTH_ROWS, KTH_ROWS), KTH_ROWS)
        x = sc_ref[rows, :]
        st_ref[rows, :] = jnp.where(x == NEG_INF, INT_MIN, _sort_key(x))
        return carry

    lax.fori_loop(0, ngroups, to_keys, 0)
    thr_ref[...] = jnp.maximum(_kth_largest_key(st_ref, ngroups, topk), INT_MIN + 1)


def _sample_select(sc_t, topk):
    n, nb = sc_t.shape
    assert nb == LANES and n % KTH_ROWS == 0
    whole = lambda shape: pl.BlockSpec(shape, lambda i: (0, 0))
    return pl.pallas_call(
        functools.partial(_sample_select_body, topk=topk),
        grid=(1,),
        in_specs=[whole((n, nb))],
        out_specs=whole((1, nb)),
        out_shape=jax.ShapeDtypeStruct((1, nb), I32),
        scratch_shapes=[pltpu.VMEM((n, nb), I32)],
        compiler_params=_params(("arbitrary",)),
        name="sample_select",
    )(sc_t)


def _sample_attend_body(pt_ref, thr_ref, sc_ref, qt_ref, q_ref, kn_ref, vn_ref, ck_hbm, cv_hbm, o_ref,
                        kbuf_ref, vbuf_ref, sem_ref, qb_ref, acc_ref, m_ref, l_ref, *, layer, n_pages):
    b = pl.program_id(0)
    c = pl.program_id(1)
    n_chunks = pl.num_programs(1)
    step = b * n_chunks + c
    n_steps = pl.num_programs(0) * n_chunks

    def copies(st, slot):
        seq, chunk = st // n_chunks, st % n_chunks
        out = []
        for p in range(ATT_PAGES):
            page = pt_ref[seq, chunk * ATT_PAGES + p]
            out.append(pltpu.make_async_copy(ck_hbm.at[layer, page], kbuf_ref.at[slot, p], sem_ref.at[slot]))
            out.append(pltpu.make_async_copy(cv_hbm.at[layer, page], vbuf_ref.at[slot, p], sem_ref.at[slot]))
        return out

    @pl.when(step == 0)
    def _():
        for cp in copies(0, 0):
            cp.start()

    @pl.when(step + 1 < n_steps)
    def _():
        for cp in copies(step + 1, (step + 1) % 2):
            cp.start()

    slot = step % 2
    for cp in copies(step, slot):
        cp.wait()

    @pl.when(c == 0)
    def _():
        m_ref[...] = jnp.full_like(m_ref, NEG_INF)
        l_ref[...] = jnp.zeros_like(l_ref)
        acc_ref[...] = jnp.zeros_like(acc_ref)
        qt = qt_ref[0] * HEAD_DIM ** -0.5
        for h in range(N_HEADS):
            qb_ref[h] = jnp.broadcast_to(qt[:, h:h + 1], (HEAD_DIM, LANES))

    thr = thr_ref[b]
    rows = pl.ds(pl.multiple_of(c * ATT_PAGES, ATT_PAGES), ATT_PAGES)
    sel = _sort_key(sc_ref[0, rows, :]) >= thr
    for h in range(N_HEADS):
        qb = qb_ref[h]
        s = jnp.concatenate([jnp.sum(kbuf_ref[slot, p, h] * qb, axis=0, keepdims=True)
                             for p in range(ATT_PAGES)], axis=0)
        s = jnp.where(sel, s, NEG_INF)
        m_old = m_ref[h:h + 1, :]
        m_new = jnp.maximum(m_old, jnp.max(jnp.max(s, axis=0, keepdims=True), axis=1, keepdims=True))
        m_safe = jnp.where(m_new == NEG_INF, 0.0, m_new)
        alpha = jnp.exp(m_old - m_safe)
        pr = jnp.exp(s - m_safe)
        l_ref[h:h + 1, :] = alpha * l_ref[h:h + 1, :] + jnp.sum(
            jnp.sum(pr, axis=0, keepdims=True), axis=1, keepdims=True)
        acc = alpha * acc_ref[h]
        for p in range(ATT_PAGES):
            acc = acc + pr[p:p + 1, :] * vbuf_ref[slot, p, h]
        acc_ref[h] = acc
        m_ref[h:h + 1, :] = m_new

    @pl.when(c == n_chunks - 1)
    def _():
        vn = vn_ref[0]
        s_new = jnp.sum(q_ref[0] * kn_ref[0], axis=1, keepdims=True) * HEAD_DIM ** -0.5
        new_sel = _sort_key(sc_ref[0, n_pages:n_pages + 1, 0:1]) >= thr
        s_new = jnp.where(new_sel, s_new, NEG_INF)
        m_old = m_ref[:, 0:1]
        m_fin = jnp.maximum(m_old, s_new)
        m_safe = jnp.where(m_fin == NEG_INF, 0.0, m_fin)
        alpha = jnp.exp(m_old - m_safe)
        p_new = jnp.exp(s_new - m_safe)
        num = jnp.sum(acc_ref[...], axis=2) * alpha + p_new * vn
        o_ref[0] = num / (l_ref[:, 0:1] * alpha + p_new)


def _sample_attend(page_table, thr, sc, q, k_new, v_new, cache_k_t, cache_v_t, layer):
    nb, n_pages = page_table.shape
    assert n_pages % ATT_PAGES == 0
    q3, kn3, vn3 = (a.reshape(nb, N_HEADS, HEAD_DIM) for a in (q, k_new, v_new))
    qt3 = q3.transpose(0, 2, 1)
    blk = pl.BlockSpec((1, N_HEADS, HEAD_DIM), lambda b, c, *_: (b, 0, 0))
    hbm = pl.BlockSpec(memory_space=pl.ANY)
    page_buf = pltpu.VMEM((2, ATT_PAGES, N_HEADS, HEAD_DIM, PAGE_SIZE), F32)
    out = pl.pallas_call(
        functools.partial(_sample_attend_body, layer=layer, n_pages=n_pages),
        grid_spec=pltpu.PrefetchScalarGridSpec(
            num_scalar_prefetch=2,
            grid=(nb, n_pages // ATT_PAGES),
            in_specs=[pl.BlockSpec((1,) + sc.shape[1:], lambda b, c, *_: (b, 0, 0)),
                      pl.BlockSpec((1, HEAD_DIM, N_HEADS), lambda b, c, *_: (b, 0, 0)),
                      blk, blk, blk, hbm, hbm],
            out_specs=blk,
            scratch_shapes=[page_buf, page_buf, pltpu.SemaphoreType.DMA((2,)),
                            pltpu.VMEM((N_HEADS, HEAD_DIM, LANES), F32),
                            pltpu.VMEM((N_HEADS, HEAD_DIM, LANES), F32),
                            pltpu.VMEM((N_HEADS, LANES), F32), pltpu.VMEM((N_HEADS, LANES), F32)],
        ),
        out_shape=jax.ShapeDtypeStruct((nb, N_HEADS, HEAD_DIM), F32),
        compiler_params=_params(("arbitrary", "arbitrary")),
        name="sample_attend",
    )(page_table, thr, sc, qt3, q3, kn3, vn3, cache_k_t, cache_v_t)
    return out.reshape(nb, ATT_WIDTH)


def _attn_sample(z, page_table, cache_k, cache_v, cache_ik, layer, topk):
    nb = page_table.shape[0]
    ki_new = z["ka"][:, :IDX_DIM]
    wi = z["wb"][:, :IDX_HEADS]
    sc = _sample_scores(page_table, z["qi"], wi, ki_new, cache_ik.transpose(0, 1, 3, 2), layer)
    thr = _sample_select(sc.reshape(nb, -1).T, topk)
    return _sample_attend(page_table, thr.reshape(nb), sc, z["q"], z["k"], z["v"],
                          cache_k.transpose(0, 1, 3, 4, 2), cache_v.transpose(0, 1, 3, 4, 2), layer)


S5_SUB = 64
S5_CHUNK = SUBLANES * S5_SUB
S5_STRIP = 512


def _s5_prep_body(are_ref, aim_ref, ldt_ref, bre_ref, bim_ref, cre_ref, cim_ref,
                  bbh_ref, bbl_ref, cc_ref, abar_ref, abarl_ref):
    ar = are_ref[...]
    ai = aim_ref[...]
    dt = jnp.exp(ldt_ref[...])
    mag = jnp.exp(dt * ar)
    abr = mag * jnp.cos(dt * ai)
    abi = mag * jnp.sin(dt * ai)
    den = ar * ar + ai * ai
    xr = abr - 1.0
    fre = (xr * ar + abi * ai) / den
    fim = (abi * ar - xr * ai) / den
    bre = bre_ref[...]
    bim = bim_ref[...]
    bb_re = fre * bre - fim * bim
    bb_im = fre * bim + fim * bre
    h, lo = _split(bb_re)
    bbh_ref[:, 0:SSM_FLAT] = h
    bbl_ref[:, 0:SSM_FLAT] = lo
    h, lo = _split(bb_im)
    bbh_ref[:, SSM_FLAT:] = h
    bbl_ref[:, SSM_FLAT:] = lo
    cc_ref[0:SSM_FLAT, :] = cre_ref[...].astype(BF16)
    cc_ref[SSM_FLAT:, :] = (-cim_ref[...]).astype(BF16)
    abar_ref[0:1, :] = abr
    abar_ref[1:2, :] = abi
    pr, pi = abr, abi
    for _ in range(int(math.log2(S5_SUB))):
        pr, pi = pr * pr - pi * pi, 2.0 * pr * pi
    abarl_ref[0:1, :] = pr
    abarl_ref[1:2, :] = pi


def _s5_prep(a_re, a_im, log_dt, b_re, b_im, c_re, c_im):
    eye = jnp.eye(SSM_GROUPS, dtype=F32)
    bd_b = lambda b: jnp.einsum("gpc,gh->gchp", b, eye).reshape(SSM_WIDTH, SSM_FLAT)
    bd_c = lambda c: jnp.einsum("gcp,gh->gphc", c, eye).reshape(SSM_FLAT, SSM_WIDTH)
    row = lambda a: a.reshape(1, SSM_FLAT)
    ldt = jnp.repeat(log_dt, SSM_STATE).reshape(1, SSM_FLAT)
    args = (row(a_re), row(a_im), ldt, bd_b(b_re), bd_b(b_im), bd_c(c_re), bd_c(c_im))
    return pl.pallas_call(
        _s5_prep_body,
        out_shape=[jax.ShapeDtypeStruct((SSM_WIDTH, 2 * SSM_FLAT), BF16),
                   jax.ShapeDtypeStruct((SSM_WIDTH, 2 * SSM_FLAT), BF16),
                   jax.ShapeDtypeStruct((2 * SSM_FLAT, SSM_WIDTH), BF16),
                   jax.ShapeDtypeStruct((2, SSM_FLAT), F32),
                   jax.ShapeDtypeStruct((2, SSM_FLAT), F32)],
        compiler_params=_params(None),
        name="s5_prep",
    )(*args)


def _s5_scan_body(u_ref, bbh_ref, cc_ref, abar_ref, abarl_ref, d_ref, y_ref, st_ref,
                  buf_ref, ends_ref, carry_ref):
    @pl.when(pl.program_id(0) == 0)
    def _():
        carry_ref[...] = jnp.zeros_like(carry_ref)

    u = u_ref[...]
    buf_ref[...] = _dot(u.astype(BF16), bbh_ref[...])

    def sweep(store):
        for s in range(SSM_FLAT // S5_STRIP):
            re = slice(s * S5_STRIP, (s + 1) * S5_STRIP)
            im = slice(SSM_FLAT + s * S5_STRIP, SSM_FLAT + (s + 1) * S5_STRIP)
            ar = jnp.broadcast_to(abar_ref[0:1, re], (SUBLANES, S5_STRIP))
            ai = jnp.broadcast_to(abar_ref[1:2, re], (SUBLANES, S5_STRIP))

            def step(i, c):
                sr, si = c
                rows = pl.ds(pl.multiple_of(i * SUBLANES, SUBLANES), SUBLANES)
                nr = ar * sr - ai * si + buf_ref[rows, re]
                ni = ar * si + ai * sr + buf_ref[rows, im]
                if store:
                    buf_ref[rows, re] = nr
                    buf_ref[rows, im] = ni
                return nr, ni

            if store:
                init = (ends_ref[:, re], ends_ref[:, im])
            else:
                init = (jnp.zeros((SUBLANES, S5_STRIP), F32),) * 2
            sr, si = lax.fori_loop(0, S5_SUB, step, init, unroll=4)
            if not store:
                ends_ref[:, re] = sr
                ends_ref[:, im] = si

    sweep(False)
    pr = abarl_ref[0:1, :]
    pi = abarl_ref[1:2, :]
    cr = carry_ref[0:1, 0:SSM_FLAT]
    ci = carry_ref[0:1, SSM_FLAT:]
    for j in range(SUBLANES):
        er = ends_ref[j:j + 1, 0:SSM_FLAT]
        ei = ends_ref[j:j + 1, SSM_FLAT:]
        ends_ref[j:j + 1, 0:SSM_FLAT] = cr
        ends_ref[j:j + 1, SSM_FLAT:] = ci
        cr, ci = pr * cr - pi * ci + er, pr * ci + pi * cr + ei
    carry_ref[0:1, 0:SSM_FLAT] = cr
    carry_ref[0:1, SSM_FLAT:] = ci
    sweep(True)
    y_ref[...] = _dot(buf_ref[...].astype(BF16), cc_ref[...]) + d_ref[...] * u
    st_ref[...] = carry_ref[0:1, :]


def _s5_prompt(us_perm, bbh, cc, abar, abarl, d_row):
    t = us_perm.shape[0]
    full = lambda a: pl.BlockSpec(a.shape, lambda i: (0,) * a.ndim)
    return pl.pallas_call(
        _s5_scan_body,
        grid=(t // S5_CHUNK,),
        in_specs=[pl.BlockSpec((S5_CHUNK, SSM_WIDTH), lambda i: (i, 0)),
                  full(bbh), full(cc), full(abar), full(abarl), full(d_row)],
        out_specs=[pl.BlockSpec((S5_CHUNK, SSM_WIDTH), lambda i: (i, 0)),
                   pl.BlockSpec((1, 2 * SSM_FLAT), lambda i: (0, 0))],
        out_shape=[jax.ShapeDtypeStruct((t, SSM_WIDTH), F32),
                   jax.ShapeDtypeStruct((1, 2 * SSM_FLAT), F32)],
        scratch_shapes=[pltpu.VMEM((S5_CHUNK, 2 * SSM_FLAT), F32),
                        pltpu.VMEM((SUBLANES, 2 * SSM_FLAT), F32),
                        pltpu.VMEM((SUBLANES, 2 * SSM_FLAT), F32)],
        compiler_params=_params(("arbitrary",)),
        name="s5_scan",
    )(us_perm, bbh, cc, abar, abarl, d_row)


def _s5_step_body(u_ref, hre_ref, him_ref, bbh_ref, bbl_ref, cc_ref, abar_ref, d_ref,
                  y_ref, ore_ref, oim_ref):
    u = u_ref[...]
    uh, ul = _split(u)
    bu = _dot(uh, bbh_ref[...]) + _dot(ul, bbh_ref[...]) + _dot(uh, bbl_ref[...])
    ar = abar_ref[0:1, :]
    ai = abar_ref[1:2, :]
    hr = hre_ref[...]
    hi = him_ref[...]
    nr = ar * hr - ai * hi + bu[:, 0:SSM_FLAT]
    ni = ar * hi + ai * hr + bu[:, SSM_FLAT:]
    ore_ref[...] = nr
    oim_ref[...] = ni
    hcat = jnp.concatenate([nr, ni], axis=1).astype(BF16)
    y_ref[...] = _dot(hcat, cc_ref[...]) + d_ref[...] * u


def _s5_sample(us, h_re, h_im, bbh, bbl, cc, abar, d_row):
    n = us.shape[0]
    return pl.pallas_call(
        _s5_step_body,
        out_shape=[jax.ShapeDtypeStruct((n, SSM_WIDTH), F32),
                   jax.ShapeDtypeStruct((n, SSM_FLAT), F32),
                   jax.ShapeDtypeStruct((n, SSM_FLAT), F32)],
        compiler_params=_params(None),
        name="s5_step",
    )(us, h_re, h_im, bbh, bbl, cc, abar, d_row)


def _s5_permute(x, inverse=False):
    t, w = x.shape
    a, b = (S5_SUB, SUBLANES) if inverse else (SUBLANES, S5_SUB)
    return x.reshape(t // S5_CHUNK, a, b, w).transpose(0, 2, 1, 3).reshape(t, w)


def _mix_body(att_ref, y_ref, ga_ref, gb_ref, h_ref, watt_ref, wglu_ref, wout_ref, gffn_ref,
              h1_ref, unh_ref, unl_ref):
    branch_a = _dot(att_ref[...].astype(BF16), watt_ref[...])
    zs = _gelu(y_ref[...]).astype(BF16)
    glu = _dot(zs, wglu_ref[...])
    branch_b = glu[:, 0:D_MODEL] * _sigmoid(glu[:, D_MODEL:])
    mixed = _sigmoid(ga_ref[...]) * branch_a + _sigmoid(gb_ref[...]) * branch_b
    h1 = h_ref[...] + _dot(mixed.astype(BF16), wout_ref[...])
    h1_ref[...] = h1
    un_t = _rmsnorm(h1, gffn_ref[...]).T
    hi, lo = _split(un_t)
    unh_ref[...] = hi
    unl_ref[...] = lo


def _mix(att, y, ga, gb, h, watt, wglu, wout, gffn, tb):
    t = h.shape[0]
    row = lambda n: pl.BlockSpec((tb, n), lambda i: (i, 0))
    full = lambda a: pl.BlockSpec(a.shape, lambda i: (0,) * a.ndim)
    col = pl.BlockSpec((D_MODEL, tb), lambda i: (0, i))
    return pl.pallas_call(
        _mix_body,
        grid=(t // tb,),
        in_specs=[row(512), row(512), row(1024), row(1024), row(1024),
                  full(watt), full(wglu), full(wout), full(gffn)],
        out_specs=[row(1024), col, col],
        out_shape=[jax.ShapeDtypeStruct((t, D_MODEL), F32),
                   jax.ShapeDtypeStruct((D_MODEL, t), BF16),
                   jax.ShapeDtypeStruct((D_MODEL, t), BF16)],
        compiler_params=_params(("arbitrary",)),
        name="mix",
    )(att, y, ga, gb, h, watt, wglu, wout, gffn)


PEER_ET = 1024
PEER_I0 = PEER_ET // PEER_NKEYS
PEER_HJ = PEER_HEADS * 2


def _top_extract(xs, iota, rounds, emit):
    xs = list(xs)
    n = xs[0].shape[0]
    for r in range(rounds):
        for k, x in enumerate(xs):
            m = jnp.max(x, axis=0, keepdims=True)
            emit(k, r, m)
            if r + 1 < rounds:
                first = jnp.min(jnp.where(x == m, iota, n), axis=0, keepdims=True)
                xs[k] = jnp.where(iota == first, NEG_INF, x)


_PAIR_ROWS = ((0, 16), (1, 8), (2, 5), (3, 4), (4, 3), (5, 2), (6, 2), (7, 2))


def _pair_candidates(sv0, sv1):
    blocks = []
    for a, lim in _PAIR_ROWS:
        n = 16 if lim > SUBLANES else SUBLANES
        blk = sv0[a:a + 1] + sv1[0:n]
        if lim < n:
            blk = jnp.where(lax.broadcasted_iota(I32, blk.shape, 0) < lim, blk, NEG_INF)
        blocks.append(blk)
    blocks.append(sv0[SUBLANES:2 * SUBLANES] + sv1[0:1])
    return jnp.concatenate(blocks, axis=0)


def _peer_body(unh_ref, unl_ref, h1_ref, wqh_ref, wql_ref, keys4_ref, u_ref, vt_ref, o_ref,
               s_ref, e_ref, sv_ref, thr_ref, rz_ref, a_ref, g_ref, acc_ref, *, tb):
    j = pl.program_id(1)
    nlt = tb // LANES

    def lane_tile(lt):
        return pl.ds(pl.multiple_of(lt * LANES, LANES), LANES)

    @pl.when(j == 0)
    def _():
        unh = unh_ref[...]
        wqh = wqh_ref[...]
        qt = _dot(wqh, unh) + _dot(wqh, unl_ref[...]) + _dot(wql_ref[...], unh)
        for hj in range(PEER_HJ):
            xh, xl = _split(qt[PEER_HALF * hj:PEER_HALF * (hj + 1)])
            s_ref[hj] = _dot(keys4_ref[hj % 2], jnp.concatenate([xh, xl, xh, xl], axis=0))

        iota1 = lax.broadcasted_iota(I32, (PEER_NKEYS, LANES), 0)

        def stage1(idx, carry):
            lanes = lane_tile(idx % nlt)
            hjs = (2 * (idx // nlt), 2 * (idx // nlt) + 1)
            xs = [s_ref[hj, :, lanes] for hj in hjs]

            def emit(k, r, m):
                sv_ref[hjs[k], r:r + 1, lanes] = m

            _top_extract(xs, iota1, PEER_TOPK, emit)
            for hj, x in zip(hjs, xs):
                e_ref[hj, :, lanes] = jnp.exp(x - sv_ref[hj, 0:1, lanes])
            return carry

        lax.fori_loop(0, PEER_HEADS * nlt, stage1, 0)

        n_cand = SUBLANES * (len(_PAIR_ROWS) + 2)
        iota2 = lax.broadcasted_iota(I32, (n_cand, LANES), 0)

        def stage2(idx, carry):
            lanes = lane_tile(idx % nlt)
            hs = (2 * (idx // nlt), 2 * (idx // nlt) + 1)
            cands = [_pair_candidates(sv_ref[2 * h, :, lanes], sv_ref[2 * h + 1, :, lanes]) for h in hs]
            tops = ([], [])
            _top_extract(cands, iota2, PEER_TOPK, lambda k, r, m: tops[k].append(m))
            for h, top in zip(hs, tops):
                z = sum(jnp.exp(c - top[0]) for c in top)
                thr_ref[h, :, lanes] = top[-1]
                rz_ref[h, :, lanes] = 1.0 / z
            return carry

        lax.fori_loop(0, PEER_HEADS // 2 * nlt, stage2, 0)
        acc_ref[...] = jnp.zeros_like(acc_ref)

    half_w = min(tb, 2 * LANES)
    halves = [slice(k * half_w, (k + 1) * half_w) for k in range(tb // half_w)]
    for cols in halves:
        a_ref[:, cols] = _dot(u_ref[...], unh_ref[:, cols])

    def dense(lt):
        lanes = slice(lt * LANES, (lt + 1) * LANES)
        grp = pl.ds(pl.multiple_of(j * PEER_I0, SUBLANES), PEER_I0)
        s0g = [s_ref[2 * h, grp, lanes] for h in range(PEER_HEADS)]
        e0g = [e_ref[2 * h, grp, lanes] * rz_ref[h, :, lanes] for h in range(PEER_HEADS)]
        for r in range(PEER_I0):
            w = jnp.zeros((PEER_NKEYS, LANES), F32)
            for h in range(PEER_HEADS):
                sel = (s_ref[2 * h + 1, :, lanes] + s0g[h][r:r + 1]) >= thr_ref[h, :, lanes]
                w = w + jnp.where(sel, e_ref[2 * h + 1, :, lanes] * e0g[h][r:r + 1], 0.0)
            rows = slice(r * PEER_NKEYS, (r + 1) * PEER_NKEYS)
            g_ref[rows, lanes] = (_gelu(a_ref[rows, lanes]) * w).astype(BF16)

    for cols in halves:
        for lt in range(cols.start // LANES, cols.stop // LANES):
            dense(lt)
        acc_ref[:, cols] += _dot(vt_ref[...], g_ref[:, cols])

    @pl.when(j == pl.num_programs(1) - 1)
    def _():
        o_ref[...] = h1_ref[...] + acc_ref[...].T


def _peer(unh, unl, h1, wqh, wql, keys4, u_bf, vt_bf, tb):
    t = h1.shape[0]
    col = pl.BlockSpec((D_MODEL, tb), lambda i, j: (0, i))
    full = lambda a: pl.BlockSpec(a.shape, lambda i, j: (0,) * a.ndim)
    vm = lambda shape, dt: pltpu.VMEM(shape, dt)
    return pl.pallas_call(
        functools.partial(_peer_body, tb=tb),
        grid=(t // tb, PEER_EXPERTS // PEER_ET),
        in_specs=[col, col, pl.BlockSpec((tb, D_MODEL), lambda i, j: (i, 0)),
                  full(wqh), full(wql), full(keys4),
                  pl.BlockSpec((PEER_ET, D_MODEL), lambda i, j: (j, 0)),
                  pl.BlockSpec((D_MODEL, PEER_ET), lambda i, j: (0, j))],
        out_specs=pl.BlockSpec((tb, D_MODEL), lambda i, j: (i, 0)),
        out_shape=jax.ShapeDtypeStruct((t, D_MODEL), F32),
        scratch_shapes=[vm((PEER_HJ, PEER_NKEYS, tb), F32), vm((PEER_HJ, PEER_NKEYS, tb), F32),
                        vm((PEER_HJ, PEER_TOPK, tb), F32), vm((PEER_HEADS, 1, tb), F32),
                        vm((PEER_HEADS, 1, tb), F32), vm((PEER_ET, tb), F32), vm((PEER_ET, tb), BF16),
                        vm((D_MODEL, tb), F32)],
        compiler_params=_params(("arbitrary", "arbitrary")),
        name="peer",
    )(unh, unl, h1, wqh, wql, keys4, u_bf, vt_bf)


def _prep_peer_weights(peer_wq, peer_keys, peer_u, peer_v):
    wqt = peer_wq.T
    wqh = wqt.astype(BF16)
    wql = (wqt - wqh.astype(F32)).astype(BF16)
    kh = peer_keys.astype(BF16)
    kl = (peer_keys - kh.astype(F32)).astype(BF16)
    keys4 = jnp.concatenate([kh, kh, kl, kl], axis=-1)
    return wqh, wql, keys4, peer_u.astype(BF16), peer_v.T.astype(BF16)


def _ple_body(h_ref, p_ref, gple_ref, wgate_ref, wproj_ref, gfin_ref, y_ref):
    h = h_ref[...]
    gate = _sigmoid(_dot(_rmsnorm(h, gple_ref[...]).astype(BF16), wgate_ref[...]))
    h = h + gate * _dot(p_ref[...].astype(BF16), wproj_ref[...])
    y_ref[...] = _rmsnorm(h, gfin_ref[...])


def _ple(h, p, gple, wgate, wproj, gfin, tb):
    t = h.shape[0]
    row = lambda n: pl.BlockSpec((tb, n), lambda i: (i, 0))
    full = lambda a: pl.BlockSpec(a.shape, lambda i: (0,) * a.ndim)
    return pl.pallas_call(
        _ple_body,
        grid=(t // tb,),
        in_specs=[row(D_MODEL), row(p.shape[1]), full(gple), full(wgate), full(wproj), full(gfin)],
        out_specs=row(D_MODEL),
        out_shape=jax.ShapeDtypeStruct((t, D_MODEL), F32),
        compiler_params=_params(("arbitrary",)),
        name="ple_final",
    )(h, p, gple, wgate, wproj, gfin)


def kernel(x_prompt, x_sample, p_prompt, p_sample, cache_k, cache_v, cache_idx_k, state_ssm_re, state_ssm_im, page_table, norm_mix, w_in, ssm_a_re, ssm_a_im, ssm_log_dt, ssm_b_re, ssm_b_im, ssm_c_re, ssm_c_im, ssm_d, w_glu, w_att_br, w_out, norm_ffn, peer_wq, peer_keys, peer_u, peer_v, norm_ple, ple_gate, ple_proj, norm_final):
    l = 0
    whi, wlo, wfast = _prep_in_weights(w_in[l])
    g = norm_mix[l][None, :]
    zp = _in_projection(x_prompt[0], g, whi, wlo, wfast, 256)
    zs = _in_projection(x_sample[:, 0], g, whi, wlo, wfast, 128)
    s5p, s5s = _s5_all(zp["us"], zs["us"], state_ssm_re[l], state_ssm_im[l], ssm_a_re[l], ssm_a_im[l],
                       ssm_log_dt[l], ssm_b_re[l], ssm_b_im[l], ssm_c_re[l], ssm_c_im[l], ssm_d[l])
    seq = x_prompt.shape[1]
    att_p = _attn_prompt(zp["qi"], zp["wb"], zp["q"], zp["ki4"], zp["kbf"], zp["vt"], min(TOPK_MAX, seq // 4))
    watt, wglu, wout = (w.astype(BF16) for w in (w_att_br[l], w_glu[l], w_out[l]))
    pw = _prep_peer_weights(peer_wq[l], peer_keys[l], peer_u[l], peer_v[l])
    wgate, wproj = ple_gate[l].astype(BF16), ple_proj[l].astype(BF16)

    def tail(att, y_ssm, z, h, p, tb):
        h1, unh, unl = _mix(att, y_ssm, z["ga"], z["gb"], h, watt, wglu, wout, norm_ffn[l][None, :], tb)
        h2 = _peer(unh, unl, h1, *pw, tb)
        return _ple(h2, p, norm_ple[l][None, :], wgate, wproj, norm_final[None, :], tb)

    y_p = tail(att_p, s5p[0], zp, x_prompt[0], p_prompt[l, 0], 512)
    nb = x_sample.shape[0]
    past = page_table.shape[1] * PAGE_SIZE
    att_s = _attn_sample(zs, page_table, cache_k, cache_v, cache_idx_k, l, min(TOPK_MAX, (past + 1) // 4))
    y_s = tail(att_s, s5s[0], zs, x_sample[:, 0], p_sample[l, :, 0], 128)

    heads = lambda a, lead: a.reshape(lead + (N_HEADS, HEAD_DIM))
    st = lambda a, lead: a.reshape(lead + (SSM_GROUPS, SSM_STATE))
    return (y_p[None], y_s[:, None],
            heads(zp["k"], (1, 1, seq)), heads(zp["v"], (1, 1, seq)), zp["ka"][:, :IDX_DIM].reshape(1, 1, seq, IDX_DIM),
            heads(zs["k"], (1, nb, 1)), heads(zs["v"], (1, nb, 1)), zs["ka"][:, :IDX_DIM].reshape(1, nb, 1, IDX_DIM),
            st(s5p[1], (1, 1)), st(s5p[2], (1, 1)), st(s5s[1], (1, nb)), st(s5s[2], (1, nb)))


def _s5_all(us_p, us_s, h_re, h_im, a_re, a_im, log_dt, b_re, b_im, c_re, c_im, d):
    bbh, bbl, cc, abar, abarl = _s5_prep(a_re, a_im, log_dt, b_re, b_im, c_re, c_im)
    d_row = d.reshape(1, SSM_WIDTH)
    y_perm, st = _s5_prompt(_s5_permute(us_p), bbh, cc, abar, abarl, d_row)
    y_p = _s5_permute(y_perm, inverse=True)
    n = us_s.shape[0]
    y_s, sre, sim = _s5_sample(us_s, h_re.reshape(n, SSM_FLAT), h_im.reshape(n, SSM_FLAT),
                               bbh, bbl, cc, abar, d_row)
    return (y_p, st[:, :SSM_FLAT], st[:, SSM_FLAT:]), (y_s, sre, sim)
```
